```python
import jax, jax.numpy as jnp
from jax import lax
import numpy as np

D_MODEL = 1024
BATCH = 32
SEQ = 2048
DEPTH = 1
DEC_BATCH = 128
DEC_SEQ = 1
PAST_LEN = 8192
PAGE_SIZE = 128

HEAD_DIM = 64
N_HEADS = D_MODEL // HEAD_DIM
SB_HEADS = N_HEADS // 2
MLP_HEADS = N_HEADS - SB_HEADS
SB_W = SB_HEADS * HEAD_DIM
MLP_W = MLP_HEADS * HEAD_DIM
IN_W = 3 * SB_W + 2 * MLP_W
CHUNK = 128
Q_BLOCK = 128
SB_BIAS_INIT = -8.0
D_FF = ((8 * D_MODEL // 3 + 255) // 256) * 256
N_SUB = 3
FFN_RES = 0.5
EPS = 1e-6

kernel_name = "hybrid_stickbreak_chunkmlp_step"


def rmsnorm(x, g):
    xf = x.astype(jnp.float32)
    y = xf * lax.rsqrt(jnp.mean(xf * xf, axis=-1, keepdims=True) + EPS)
    return (y * g.astype(jnp.float32)).astype(x.dtype)


def swiglu(h, w_gu, w_dn):
    g, u = jnp.split(h @ w_gu, 2, axis=-1)
    return (jax.nn.silu(g) * u) @ w_dn


def stick_breaking(q, k, v, bias, q_start):
    tq, n_keys = q.shape[1], k.shape[1]
    z = (jnp.einsum('bqhd,bkhd->bhqk', q, k).astype(jnp.float32) * (HEAD_DIM ** -0.5)
         + bias.astype(jnp.float32)[None, :, None, None])
    qpos = q_start + jnp.arange(tq)
    kpos = jnp.arange(n_keys)
    causal = kpos[None, :] < qpos[:, None]
    log_1mb = jnp.where(causal, jax.nn.log_sigmoid(-z), 0.0)
    between = lax.cumsum(log_1mb, axis=3, reverse=True) - log_1mb
    a = jnp.where(causal, jnp.exp(jax.nn.log_sigmoid(z) + between), 0.0)
    return jnp.einsum('bhqk,bkhd->bqhd', a.astype(v.dtype), v)


def chunk_spatial_gate(u, v, w_s, b_s):
    b, t, h, d = v.shape
    n_chunks = -(-t // CHUNK)
    pad = n_chunks * CHUNK - t
    vp = jnp.pad(v, ((0, 0), (0, pad), (0, 0), (0, 0))).reshape(b, n_chunks, CHUNK, h, d)
    w = w_s * jnp.tril(jnp.ones((CHUNK, CHUNK), w_s.dtype))
    mixed = jnp.einsum('hts,bnshd->bnthd', w, vp) + b_s.T[None, None, :, :, None]
    return u * mixed.reshape(b, n_chunks * CHUNK, h, d)[:, :t]


def split_heads(h, w_in, mlp_v_g):
    b, t, _ = h.shape
    q, k, v, u, gv = jnp.split(h @ w_in, [SB_W, 2 * SB_W, 3 * SB_W, 3 * SB_W + MLP_W], axis=-1)
    q = q.reshape(b, t, SB_HEADS, HEAD_DIM)
    k = k.reshape(b, t, SB_HEADS, HEAD_DIM)
    v = v.reshape(b, t, SB_HEADS, HEAD_DIM)
    u = u.reshape(b, t, MLP_HEADS, HEAD_DIM)
    gv = rmsnorm(gv.reshape(b, t, MLP_HEADS, HEAD_DIM), mlp_v_g)
    return q, k, v, u, gv


def merge_heads(sb_o, mg_o, sb_out_g, mlp_out_g, w_out):
    b, t = sb_o.shape[:2]
    o = jnp.concatenate([rmsnorm(sb_o, sb_out_g).reshape(b, t, SB_W),
                         rmsnorm(mg_o, mlp_out_g).reshape(b, t, MLP_W)], axis=-1)
    return o @ w_out


def decoder_layer(x, c, mix_fn, w_c, b_c, norm_g, w1_gu, w1_dn, w2_gu, w2_dn):
    mod = (jax.nn.silu(c) @ w_c + b_c).reshape(c.shape[0], 1, N_SUB, 3, D_MODEL)
    shift, scale, gate = mod[:, :, :, 0], mod[:, :, :, 1], mod[:, :, :, 2]

    def modulate(x, i):
        return rmsnorm(x, norm_g[2 * i]) * (1 + scale[:, :, i]) + shift[:, :, i]

    def residual(x, i, out, w):
        return x + w * gate[:, :, i] * rmsnorm(out, norm_g[2 * i + 1])

    x = residual(x, 0, swiglu(modulate(x, 0), w1_gu, w1_dn), FFN_RES)
    mix_out, new_state = mix_fn(modulate(x, 1))
    x = residual(x, 1, mix_out, 1.0)
    x = residual(x, 2, swiglu(modulate(x, 2), w2_gu, w2_dn), FFN_RES)
    return x, new_state


def setup_inputs(seed: int = 0) -> dict:
    key = jax.random.key(seed)
    ks = jax.random.split(key, 24)
    n_pages = PAST_LEN // PAGE_SIZE
    n_phys = (5 * DEC_BATCH * n_pages) // 4
    nrm = lambda k, shape, s: jax.random.normal(k, shape, jnp.float32) * s
    page_table = jax.random.permutation(ks[6], n_phys)[:DEC_BATCH * n_pages]
    page_table = page_table.reshape(DEC_BATCH, n_pages).astype(jnp.int32)
    return {
        'x_prompt': nrm(ks[0], (BATCH, SEQ, D_MODEL), 1.0),
        'x_sample': nrm(ks[1], (DEC_BATCH, DEC_SEQ, D_MODEL), 1.0),
        'c_prompt': nrm(ks[2], (BATCH, D_MODEL), 1.0),
        'c_sample': nrm(ks[3], (DEC_BATCH, D_MODEL), 1.0),
        'cache_k': nrm(ks[4], (DEPTH, n_phys, PAGE_SIZE, SB_HEADS, HEAD_DIM), 1.0),
        'cache_v': nrm(ks[5], (DEPTH, n_phys, PAGE_SIZE, SB_HEADS, HEAD_DIM), 1.0),
        'page_table': page_table,
        'w_c': nrm(ks[7], (DEPTH, D_MODEL, N_SUB * 3 * D_MODEL), D_MODEL ** -0.5),
        'b_c': nrm(ks[8], (DEPTH, N_SUB * 3 * D_MODEL), 0.02),
        'norm_g': 1.0 + nrm(ks[9], (DEPTH, 2 * N_SUB, D_MODEL), 0.05),
        'w_in': nrm(ks[10], (DEPTH, D_MODEL, IN_W), D_MODEL ** -0.5),
        'sb_bias': SB_BIAS_INIT + nrm(ks[21], (DEPTH, SB_HEADS), 0.5),
        'mlp_v_g': 1.0 + nrm(ks[11], (DEPTH, MLP_HEADS, HEAD_DIM), 0.05),
        'w_s': nrm(ks[12], (DEPTH, MLP_HEADS, CHUNK, CHUNK), CHUNK ** -0.5),
        'b_s': 1.0 + nrm(ks[13], (DEPTH, MLP_HEADS, CHUNK), 0.1),
        'sb_out_g': 1.0 + nrm(ks[14], (DEPTH, SB_HEADS, HEAD_DIM), 0.05),
        'mlp_out_g': 1.0 + nrm(ks[15], (DEPTH, MLP_HEADS, HEAD_DIM), 0.05),
        'w_out': nrm(ks[16], (DEPTH, D_MODEL, D_MODEL), D_MODEL ** -0.5),
        'w1_gu': nrm(ks[17], (DEPTH, D_MODEL, 2 * D_FF), D_MODEL ** -0.5),
        'w1_dn': nrm(ks[18], (DEPTH, D_FF, D_MODEL), D_FF ** -0.5),
        'w2_gu': nrm(ks[19], (DEPTH, D_MODEL, 2 * D_FF), D_MODEL ** -0.5),
        'w2_dn': nrm(ks[20], (DEPTH, D_FF, D_MODEL), D_FF ** -0.5),
    }


def reference(x_prompt, x_sample, c_prompt, c_sample, cache_k, cache_v, page_table,
              w_c, b_c, norm_g, w_in, sb_bias, mlp_v_g, w_s, b_s, sb_out_g, mlp_out_g, w_out,
              w1_gu, w1_dn, w2_gu, w2_dn):
    xp, xs = x_prompt, x_sample
    kp_rows, vp_rows, ks_rows, vs_rows, gvs_rows = [], [], [], [], []
    for l in range(DEPTH):
        def prompt_mix(h, l=l):
            q, k, v, u, gv = split_heads(h, w_in[l], mlp_v_g[l])
            t = h.shape[1]
            sb = jnp.concatenate(
                [stick_breaking(q[:, s:s + Q_BLOCK], k[:, :s + Q_BLOCK], v[:, :s + Q_BLOCK],
                                sb_bias[l], s)
                 for s in range(0, t, Q_BLOCK)], axis=1)
            mg = chunk_spatial_gate(u, gv, w_s[l], b_s[l])
            return merge_heads(sb, mg, sb_out_g[l], mlp_out_g[l], w_out[l]), (k, v)

        def sample_mix(h, l=l):
            q, k, v, u, gv = split_heads(h, w_in[l], mlp_v_g[l])
            nb = h.shape[0]
            past = page_table.shape[1] * PAGE_SIZE
            pk = cache_k[l][page_table].reshape(nb, past, SB_HEADS, HEAD_DIM)
            pv = cache_v[l][page_table].reshape(nb, past, SB_HEADS, HEAD_DIM)
            sb = stick_breaking(q, jnp.concatenate([pk, k], axis=1),
                                jnp.concatenate([pv, v], axis=1), sb_bias[l], past)
            mg = chunk_spatial_gate(u, gv, w_s[l], b_s[l])
            return merge_heads(sb, mg, sb_out_g[l], mlp_out_g[l], w_out[l]), (k, v, gv)

        xp, (kp, vp) = decoder_layer(xp, c_prompt, prompt_mix, w_c[l], b_c[l], norm_g[l],
                                     w1_gu[l], w1_dn[l], w2_gu[l], w2_dn[l])
        xs, (ks_, vs_, gvs) = decoder_layer(xs, c_sample, sample_mix, w_c[l], b_c[l], norm_g[l],
                                            w1_gu[l], w1_dn[l], w2_gu[l], w2_dn[l])
        kp_rows.append(kp)
        vp_rows.append(vp)
        ks_rows.append(ks_)
        vs_rows.append(vs_)
        gvs_rows.append(gvs)
    k_prompt = jnp.stack(kp_rows)
    v_prompt = jnp.stack(vp_rows)
    k_sample = jnp.stack(ks_rows)
    v_sample = jnp.stack(vs_rows)
    gv_sample = jnp.stack(gvs_rows)
    return (xp, xs, k_prompt, v_prompt, k_sample, v_sample, gv_sample)
```

```python
import functools

import jax
import jax.numpy as jnp
from jax import lax
from jax.experimental import pallas as pl
from jax.experimental.pallas import tpu as pltpu

F32 = jnp.float32
BF16 = jnp.bfloat16

HEAD_DIM = 64
CHUNK = 128
PAGE_SIZE = 128
N_SUB = 3
FFN_RES = 0.5
EPS = 1e-6

LANES = 128
MXU_DIM = 256
VMEM_LIMIT_BYTES = 56 * 1024 * 1024

ROW_TILE = 512
KEY_BLOCK = MXU_DIM
ATTN_TILE = 2 * KEY_BLOCK
PAGES_PER_STEP = 16
LOG2E = 1.4426950408889634


def _dot(a, b):
    return jnp.dot(a, b, preferred_element_type=F32)


def _dot_nt(a, b):
    return lax.dot_general(a, b, (((1,), (1,)), ((), ())), preferred_element_type=F32)


def _split_dot(x, m):
    hi = x.astype(BF16)
    lo = (x - hi.astype(F32)).astype(BF16)
    return _dot(hi, m) + _dot(lo, m)


def _sigmoid(x):
    return 1.0 / (1.0 + jnp.exp(-x))


def _rms(x, g):
    ms = jnp.mean(x * x, axis=-1, keepdims=True)
    return x * lax.rsqrt(ms + EPS) * g


def _head_rms(x, g, seg):
    ss = _split_dot(x * x, seg)
    return x * lax.rsqrt(ss * (1.0 / HEAD_DIM) + EPS) * g


def _log2_sigmoids(y):
    log_b = jnp.minimum(y, 0.0) - jnp.log2(1.0 + jnp.exp2(-jnp.abs(y)))
    return log_b, log_b - y


def _stick_weights(y, r, tmat, causal):
    log_b, log_1mb = _log2_sigmoids(y)
    n_blocks = y.shape[1] // KEY_BLOCK
    weights = [None] * n_blocks
    for i in range(n_blocks - 1, -1, -1):
        cols = slice(i * KEY_BLOCK, (i + 1) * KEY_BLOCK)
        masked = causal is not None and i == n_blocks - 1
        l1 = jnp.where(causal, log_1mb[:, cols], 0.0) if masked else log_1mb[:, cols]
        between = _dot(l1.astype(BF16), tmat) + r
        a = jnp.exp2(log_b[:, cols] + between)
        weights[i] = (jnp.where(causal, a, 0.0) if masked else a).astype(BF16)
        r = r + jnp.sum(l1, axis=-1, keepdims=True)
    return (weights[0] if n_blocks == 1 else jnp.concatenate(weights, axis=1)), r


def _suffix_ones(n):
    later = lax.broadcasted_iota(jnp.int32, (n, n), 0) > lax.broadcasted_iota(jnp.int32, (n, n), 1)
    return jnp.where(later, 1.0, 0.0).astype(BF16), later


def _ffn(x, shift, scale, gate, g_in, g_out, wgu_ref, wdn_ref, act_ref):
    d_ff = wdn_ref.shape[0]
    h = (_rms(x, g_in) * (1.0 + scale) + shift).astype(BF16)
    for c0 in range(0, d_ff, 2 * MXU_DIM):
        c1 = min(c0 + 2 * MXU_DIM, d_ff)
        g = _dot(h, wgu_ref[:, c0:c1])
        u = _dot(h, wgu_ref[:, d_ff + c0:d_ff + c1])
        act_ref[:, c0:c1] = (g * _sigmoid(g) * u).astype(BF16)
    r = _dot(act_ref[...], wdn_ref[...])
    return x + FFN_RES * gate * _rms(r, g_out)


def _mod_kernel(c_ref, w_ref, b_ref, o_ref):
    c = c_ref[...]
    o_ref[...] = _dot((c * _sigmoid(c)).astype(BF16), w_ref[...]) + b_ref[...]


def _mod_call(c_all, w_c, b_c):
    n, d = c_all.shape
    width = w_c.shape[1]
    return pl.pallas_call(
        _mod_kernel,
        grid=(width // d,),
        in_specs=[pl.BlockSpec((n, d), lambda j: (0, 0)),
                  pl.BlockSpec((d, d), lambda j: (0, j)),
                  pl.BlockSpec((1, d), lambda j: (0, j))],
        out_specs=pl.BlockSpec((n, d), lambda j: (0, j)),
        out_shape=jax.ShapeDtypeStruct((n, width), F32),
        name="mod",
    )(c_all, w_c, b_c.reshape(1, width))


def _ffn_mix_in_kernel(sample, x_ref, sh0_ref, sc0_ref, ga0_ref, sh1_ref, sc1_ref,
                       g0_ref, g1_ref, g2_ref, wgu_ref, wdn_ref, win_ref,
                       vg_ref, og_ref, seg_ref, ws_ref, bs_ref, *rest):
    if sample:
        x1_ref, k_ref, v_ref, qb_ref, gv_ref, mgn_ref, act_ref = rest
    else:
        x1_ref, k_ref, v_ref, qb_ref, kb_ref, vb_ref, mgn_ref, act_ref, mg_ref = rest
    tm = x_ref.shape[0]
    sbw = k_ref.shape[1]

    x1 = _ffn(x_ref[...], sh0_ref[...], sc0_ref[...], ga0_ref[...], g0_ref[...], g1_ref[...],
              wgu_ref, wdn_ref, act_ref)
    x1_ref[...] = x1

    h = (_rms(x1, g2_ref[...]) * (1.0 + sc1_ref[...]) + sh1_ref[...]).astype(BF16)
    qb_ref[...] = (_dot(h, win_ref[:, 0:sbw]) * (HEAD_DIM ** -0.5 * LOG2E)).astype(BF16)
    k = _dot(h, win_ref[:, sbw:2 * sbw])
    k_ref[...] = k
    v = _dot(h, win_ref[:, 2 * sbw:3 * sbw])
    v_ref[...] = v
    u = _dot(h, win_ref[:, 3 * sbw:4 * sbw])
    gv = _head_rms(_dot(h, win_ref[:, 4 * sbw:5 * sbw]), vg_ref[...], seg_ref[...])

    if sample:
        gv_ref[...] = gv
        mg = u * (gv * ws_ref[...] + bs_ref[...])
    else:
        kb_ref[...] = k.astype(BF16)
        vb_ref[...] = v.astype(BF16)
        even_head = (lax.broadcasted_iota(jnp.int32, (1, sbw), 1) // HEAD_DIM) % 2 == 0
        gv_even = jnp.where(even_head, gv, 0.0).astype(BF16)
        gv_odd = jnp.where(even_head, 0.0, gv).astype(BF16)
        row = lax.broadcasted_iota(jnp.int32, (CHUNK, CHUNK), 0)
        col = lax.broadcasted_iota(jnp.int32, (CHUNK, CHUNK), 1)
        tril = row >= col
        for p in range(sbw // LANES):
            wp = jnp.concatenate([jnp.where(tril, ws_ref[2 * p], 0.0),
                                  jnp.where(tril, ws_ref[2 * p + 1], 0.0)], axis=1).astype(BF16)
            cols = slice(p * LANES, (p + 1) * LANES)
            for r0 in range(0, tm, CHUNK):
                rows = slice(r0, r0 + CHUNK)
                stacked = jnp.concatenate([gv_even[rows, cols], gv_odd[rows, cols]], axis=0)
                mixed = _dot(wp, stacked) + bs_ref[:, cols]
                mg_ref[rows, cols] = u[rows, cols] * mixed
        mg = mg_ref[...]
    mgn_ref[...] = _head_rms(mg, og_ref[...], seg_ref[...]).astype(BF16)


def _resident(shape):
    nd = len(shape)
    return pl.BlockSpec(shape, lambda i: (0,) * nd, pipeline_mode=pl.Buffered(1))


def _mod_specs(mod, cols, tm, rows_per_mod):
    d = mod.shape[-1] // (3 * N_SUB)
    if mod.ndim == 3:
        tiles = rows_per_mod // tm
        return [pl.BlockSpec((None, 1, d), functools.partial(lambda i, c: (i // tiles, 0, c), c=c))
                for c in cols]
    return [pl.BlockSpec((tm, d), functools.partial(lambda i, c: (i, c), c=c)) for c in cols]


def _norm_specs(idx, d):
    return [pl.BlockSpec((None, 1, d), functools.partial(lambda i, c: (c, 0, 0), c=c),
                         pipeline_mode=pl.Buffered(1)) for c in idx]


def _ffn_mix_in_call(sample, x, mod, rows_per_mod, norm_g3, wgu, wdn, win, vg, og, seg, ws, bs, tm):
    n, d = x.shape
    d_ff = wdn.shape[0]
    sbw = seg.shape[0]
    row = lambda w: pl.BlockSpec((tm, w), lambda i: (i, 0))
    in_specs = ([row(d)] + _mod_specs(mod, (0, 1, 2, 3, 4), tm, rows_per_mod)
                + _norm_specs((0, 1, 2), d)
                + [_resident(a.shape) for a in (wgu, wdn, win, vg, og, seg, ws, bs)])
    f32_out = jax.ShapeDtypeStruct((n, sbw), F32)
    bf_out = jax.ShapeDtypeStruct((n, sbw), BF16)
    if sample:
        out_shape = [jax.ShapeDtypeStruct((n, d), F32), f32_out, f32_out, bf_out, f32_out, bf_out]
        scratch = [pltpu.VMEM((tm, d_ff), BF16)]
    else:
        out_shape = [jax.ShapeDtypeStruct((n, d), F32), f32_out, f32_out, bf_out, bf_out, bf_out,
                     bf_out]
        scratch = [pltpu.VMEM((tm, d_ff), BF16), pltpu.VMEM((tm, sbw), F32)]
    out_specs = [row(s.shape[1]) for s in out_shape]
    return pl.pallas_call(
        functools.partial(_ffn_mix_in_kernel, sample),
        grid=(n // tm,),
        in_specs=in_specs,
        out_specs=out_specs,
        out_shape=out_shape,
        scratch_shapes=scratch,
        compiler_params=pltpu.CompilerParams(dimension_semantics=("arbitrary",),
                                             vmem_limit_bytes=VMEM_LIMIT_BYTES),
        name="ffn_mix_in_sample" if sample else "ffn_mix_in",
    )(x, mod, mod, mod, mod, mod, norm_g3, norm_g3, norm_g3, wgu, wdn, win, vg, og, seg, ws, bs)


def _sb_prompt_kernel(bias_ref, q_ref, k_ref, v_ref, g_ref, o_ref, acc_ref, r_ref):
    seq = q_ref.shape[0]
    t, h = ATTN_TILE, KEY_BLOCK
    pair = pl.program_id(1)
    lane = lax.broadcasted_iota(jnp.int32, (1, LANES), 1)
    keep = (jnp.where(lane < HEAD_DIM, 1.0, 0.0).astype(BF16),
            jnp.where(lane < HEAD_DIM, 0.0, 1.0).astype(BF16))
    tmat, causal = _suffix_ones(h)
    si = lax.broadcasted_iota(jnp.int32, (LANES, LANES), 0) // HEAD_DIM
    sj = lax.broadcasted_iota(jnp.int32, (LANES, LANES), 1) // HEAD_DIM
    seg = jnp.where(si == sj, 1.0, 0.0).astype(BF16)
    biases = (bias_ref[2 * pair] * LOG2E, bias_ref[2 * pair + 1] * LOG2E)
    halves = (slice(0, h), slice(h, t))

    def q_tile(qi, carry):
        r0 = pl.multiple_of(qi * t, t)
        q = q_ref[pl.ds(r0, t), :]
        qms = (q * keep[0], q * keep[1])

        k = k_ref[pl.ds(r0, t), :]
        v = v_ref[pl.ds(r0, t), :]
        acc = [None, None]
        for hh in range(2):
            vm = v * keep[hh]
            for half, keys in ((0, halves[0]), (1, slice(0, t))):
                y = _dot_nt(qms[hh][halves[half]], k[keys]) + biases[hh]
                w, r = _stick_weights(y, 0.0, tmat, causal)
                pv = _dot(w, vm[keys])
                acc[half] = pv if acc[half] is None else acc[half] + pv
                r_ref[hh, halves[half], :] = r
        acc_ref[halves[0], :] = acc[0]
        acc_ref[halves[1], :] = acc[1]

        def k_tile(j, c):
            k0 = pl.multiple_of((qi - 1 - j) * t, t)
            k = k_ref[pl.ds(k0, t), :]
            v = v_ref[pl.ds(k0, t), :]
            acc = [acc_ref[halves[0], :], acc_ref[halves[1], :]]
            for hh in range(2):
                vm = v * keep[hh]
                for half in range(2):
                    y = _dot_nt(qms[hh][halves[half]], k) + biases[hh]
                    w, r = _stick_weights(y, r_ref[hh, halves[half], :], tmat, None)
                    acc[half] = acc[half] + _dot(w, vm)
                    r_ref[hh, halves[half], :] = r
            acc_ref[halves[0], :] = acc[0]
            acc_ref[halves[1], :] = acc[1]
            return c

        lax.fori_loop(0, qi, k_tile, 0)
        o_ref[pl.ds(r0, t), :] = _head_rms(acc_ref[...], g_ref[...], seg).astype(BF16)
        return carry

    lax.fori_loop(0, seq // t, q_tile, 0)


def _sb_prompt_call(qb, kb, vb, bias, g, seq):
    n, sbw = qb.shape
    blk = lambda: pl.BlockSpec((seq, LANES), lambda b, p: (b, p))
    return pl.pallas_call(
        _sb_prompt_kernel,
        grid=(n // seq, sbw // LANES),
        in_specs=[pl.BlockSpec(memory_space=pltpu.SMEM), blk(), blk(), blk(),
                  pl.BlockSpec((1, LANES), lambda b, p: (0, p))],
        out_specs=blk(),
        out_shape=jax.ShapeDtypeStruct((n, sbw), BF16),
        scratch_shapes=[pltpu.VMEM((ATTN_TILE, LANES), F32), pltpu.VMEM((2, ATTN_TILE, 1), F32)],
        compiler_params=pltpu.CompilerParams(dimension_semantics=("arbitrary", "arbitrary"),
                                             vmem_limit_bytes=VMEM_LIMIT_BYTES),
        name="sb_prompt",
    )(bias, qb, kb, vb, g)


def _sb_sample_kernel(pt_ref, bias_ref, q_ref, g_ref, seg_ref, *rest):
    pg = PAGES_PER_STEP
    k_refs, v_refs = rest[:pg], rest[pg:2 * pg]
    o_ref, acc_ref, r_ref = rest[2 * pg:]
    step = pl.program_id(1)
    n_heads, sbw = acc_ref.shape
    h = KEY_BLOCK
    n_blocks = pg * PAGE_SIZE // h

    @pl.when(step == 0)
    def _():
        acc_ref[...] = jnp.zeros_like(acc_ref)
        r_ref[...] = jnp.zeros_like(r_ref)

    head_of_row = lax.broadcasted_iota(jnp.int32, (n_heads, sbw), 0)
    head_of_col = lax.broadcasted_iota(jnp.int32, (n_heads, sbw), 1) // HEAD_DIM
    own = head_of_row == head_of_col
    q = jnp.broadcast_to(q_ref[...].astype(F32), (n_heads, sbw))
    qm = jnp.where(own, q, 0.0).astype(BF16)
    tmat, _ = _suffix_ones(h)

    keys = jnp.concatenate([k_refs[i][...].astype(BF16) for i in range(pg)], axis=0)
    y = _dot_nt(qm, keys) + bias_ref[...] * LOG2E
    log_b, log_1mb = _log2_sigmoids(y)
    stacked = jnp.concatenate([log_1mb[:, i * h:(i + 1) * h] for i in range(n_blocks)], axis=0)
    within = _dot(stacked.astype(BF16), tmat)
    totals = jnp.sum(stacked, axis=-1, keepdims=True)
    r = r_ref[...]
    between = [None] * n_blocks
    for i in range(n_blocks - 1, -1, -1):
        rows = slice(i * n_heads, (i + 1) * n_heads)
        between[i] = within[rows] + r
        r = r + totals[rows]
    a = jnp.exp2(log_b + jnp.concatenate(between, axis=1)).astype(BF16)
    vals = jnp.concatenate([v_refs[i][...].astype(BF16) for i in range(pg)], axis=0)
    acc = acc_ref[...] + _dot(a, vals)
    acc_ref[...] = acc
    r_ref[...] = r

    @pl.when(step == pl.num_programs(1) - 1)
    def _():
        o = jnp.sum(jnp.where(own, acc, 0.0), axis=0, keepdims=True)
        o_ref[...] = _head_rms(o, g_ref[...], seg_ref[...]).astype(BF16)


def _sb_sample_call(page_table, bias, qb, g, seg, pool_k, pool_v):
    nb, n_pages = page_table.shape
    sbw = qb.shape[1]
    n_heads = sbw // HEAD_DIM
    pg = PAGES_PER_STEP
    n_steps = n_pages // pg
    pool_k = pool_k.reshape(pool_k.shape[0], PAGE_SIZE, sbw)
    pool_v = pool_v.reshape(pool_v.shape[0], PAGE_SIZE, sbw)

    def page_spec(i):
        def index(b, s, pt):
            return (pt[b * n_pages + (n_steps - 1 - s) * pg + i], 0, 0)
        return pl.BlockSpec((None, PAGE_SIZE, sbw), index)

    whole = lambda shape: pl.BlockSpec(shape, lambda b, s, pt: (0,) * len(shape))
    grid_spec = pltpu.PrefetchScalarGridSpec(
        num_scalar_prefetch=1,
        grid=(nb, n_steps),
        in_specs=([whole((n_heads, 1)),
                   pl.BlockSpec((None, 1, sbw), lambda b, s, pt: (b, 0, 0)),
                   whole((1, sbw)), whole((sbw, sbw))]
                  + [page_spec(i) for i in range(pg)] * 2),
        out_specs=pl.BlockSpec((None, 1, sbw), lambda b, s, pt: (b, 0, 0)),
        scratch_shapes=[pltpu.VMEM((n_heads, sbw), F32), pltpu.VMEM((n_heads, 1), F32)],
    )
    out = pl.pallas_call(
        _sb_sample_kernel,
        grid_spec=grid_spec,
        out_shape=jax.ShapeDtypeStruct((nb, 1, sbw), BF16),
        compiler_params=pltpu.CompilerParams(dimension_semantics=("arbitrary", "arbitrary"),
                                             vmem_limit_bytes=VMEM_LIMIT_BYTES),
        name="sb_sample",
    )(page_table.reshape(-1), bias.reshape(n_heads, 1), qb.reshape(nb, 1, sbw), g, seg,
      *([pool_k] * pg), *([pool_v] * pg))
    return out.reshape(nb, sbw)


def _mix_out_ffn_kernel(x_ref, sbn_ref, mgn_ref, ga1_ref, sh2_ref, sc2_ref, ga2_ref,
                        g3_ref, g4_ref, g5_ref, wout_ref, wgu_ref, wdn_ref, y_ref, act_ref):
    sbw = sbn_ref.shape[1]
    o = _dot(sbn_ref[...], wout_ref[0:sbw, :]) + _dot(mgn_ref[...], wout_ref[sbw:, :])
    x2 = x_ref[...] + ga1_ref[...] * _rms(o, g3_ref[...])
    y_ref[...] = _ffn(x2, sh2_ref[...], sc2_ref[...], ga2_ref[...], g4_ref[...], g5_ref[...],
                      wgu_ref, wdn_ref, act_ref)


def _mix_out_ffn_call(sample, x, sbn, mgn, mod, rows_per_mod, norm_g3, wout, wgu, wdn, tm):
    n, d = x.shape
    d_ff = wdn.shape[0]
    row = lambda w: pl.BlockSpec((tm, w), lambda i: (i, 0))
    in_specs = ([row(d), row(sbn.shape[1]), row(mgn.shape[1])]
                + _mod_specs(mod, (5, 6, 7, 8), tm, rows_per_mod)
                + _norm_specs((3, 4, 5), d)
                + [_resident(a.shape) for a in (wout, wgu, wdn)])
    return pl.pallas_call(
        _mix_out_ffn_kernel,
        grid=(n // tm,),
        in_specs=in_specs,
        out_specs=row(d),
        out_shape=jax.ShapeDtypeStruct((n, d), F32),
        scratch_shapes=[pltpu.VMEM((tm, d_ff), BF16)],
        compiler_params=pltpu.CompilerParams(dimension_semantics=("arbitrary",),
                                             vmem_limit_bytes=VMEM_LIMIT_BYTES),
        name="mix_out_ffn_sample" if sample else "mix_out_ffn",
    )(x, sbn, mgn, mod, mod, mod, mod, norm_g3, norm_g3, norm_g3, wout, wgu, wdn)


def kernel(x_prompt, x_sample, c_prompt, c_sample, cache_k, cache_v, page_table, w_c, b_c, norm_g, w_in, sb_bias, mlp_v_g, w_s, b_s, sb_out_g, mlp_out_g, w_out, w1_gu, w1_dn, w2_gu, w2_dn):
    depth = w_c.shape[0]
    nb_p, seq, d = x_prompt.shape
    nb_s, dec_seq, _ = x_sample.shape
    n_sb, hd = sb_out_g.shape[1:]
    n_mlp = mlp_v_g.shape[1]
    sbw = n_sb * hd
    assert hd == HEAD_DIM and n_mlp * hd == sbw and dec_seq == 1
    assert w_s.shape[2] == CHUNK and cache_k.shape[2] == PAGE_SIZE
    assert seq % ATTN_TILE == 0 and seq % ROW_TILE == 0 and ROW_TILE % CHUNK == 0
    assert page_table.shape[1] % PAGES_PER_STEP == 0

    head_id = jnp.arange(sbw) // HEAD_DIM
    seg = (head_id[:, None] == head_id[None, :]).astype(BF16)
    flat = lambda a: a.reshape(1, sbw)

    xp = x_prompt.reshape(nb_p * seq, d)
    xs = x_sample.reshape(nb_s, d)
    c_all = jnp.concatenate([c_prompt, c_sample], axis=0)
    outs = [[] for _ in range(5)]
    for l in range(depth):
        mod = _mod_call(c_all, w_c[l].astype(BF16), b_c[l])
        mod_p = mod[:nb_p].reshape(nb_p, 1, -1)
        mod_s = mod[nb_p:]
        norm_g3 = norm_g[l].reshape(2 * N_SUB, 1, d)
        w1 = (w1_gu[l].astype(BF16), w1_dn[l].astype(BF16))
        w2 = (w2_gu[l].astype(BF16), w2_dn[l].astype(BF16))
        win = w_in[l].astype(BF16)
        wout = w_out[l].astype(BF16)
        vg, og, sg = flat(mlp_v_g[l]), flat(mlp_out_g[l]), flat(sb_out_g[l])
        bs_rows = jnp.repeat(b_s[l].T, HEAD_DIM, axis=1)
        ws0 = flat(jnp.repeat(w_s[l][:, 0, 0], HEAD_DIM))
        bs0 = bs_rows[0:1]

        x1p, kp, vp, qbp, kbp, vbp, mgnp = _ffn_mix_in_call(
            False, xp, mod_p, seq, norm_g3, *w1, win, vg, og, seg, w_s[l], bs_rows, ROW_TILE)
        x1s, ks, vs, qbs, gvs, mgns = _ffn_mix_in_call(
            True, xs, mod_s, 1, norm_g3, *w1, win, vg, og, seg, ws0, bs0, nb_s)

        sbnp = _sb_prompt_call(qbp, kbp, vbp, sb_bias[l], sg, seq)
        sbns = _sb_sample_call(page_table, sb_bias[l], qbs, sg, seg, cache_k[l], cache_v[l])

        xp = _mix_out_ffn_call(False, x1p, sbnp, mgnp, mod_p, seq, norm_g3, wout, *w2, ROW_TILE)
        xs = _mix_out_ffn_call(True, x1s, sbns, mgns, mod_s, 1, norm_g3, wout, *w2, nb_s)

        outs[0].append(kp.reshape(nb_p, seq, n_sb, hd))
        outs[1].append(vp.reshape(nb_p, seq, n_sb, hd))
        outs[2].append(ks.reshape(nb_s, 1, n_sb, hd))
        outs[3].append(vs.reshape(nb_s, 1, n_sb, hd))
        outs[4].append(gvs.reshape(nb_s, 1, n_mlp, hd))
    return (xp.reshape(nb_p, seq, d), xs.reshape(nb_s, 1, d)) + tuple(jnp.stack(o) for o in outs)
```

```python
import functools

import jax
import jax.numpy as jnp
from jax import lax
from jax.experimental import pallas as pl
from jax.experimental.pallas import tpu as pltpu

F32 = jnp.float32
BF16 = jnp.bfloat16

HEAD_DIM = 64
CHUNK = 128
PAGE_SIZE = 128
N_SUB = 3
FFN_RES = 0.5
EPS = 1e-6

LANES = 128
MXU_DIM = 256
VMEM_LIMIT_BYTES = 56 * 1024 * 1024

ROW_TILE = 512
KEY_BLOCK = MXU_DIM
ATTN_TILE = 2 * KEY_BLOCK
PAGES_PER_STEP = 16
LOG2E = 1.4426950408889634


def _dot(a, b):
    return jnp.dot(a, b, preferred_element_type=F32)


def _dot_nt(a, b):
    return lax.dot_general(a, b, (((1,), (1,)), ((), ())), preferred_element_type=F32)


def _split_dot(x, m):
    hi = x.astype(BF16)
    lo = (x - hi.astype(F32)).astype(BF16)
    return _dot(hi, m) + _dot(lo, m)


def _sigmoid(x):
    return 1.0 / (1.0 + jnp.exp(-x))


def _rms(x, g):
    ms = jnp.mean(x * x, axis=-1, keepdims=True)
    return x * lax.rsqrt(ms + EPS) * g


def _head_rms(x, g, seg):
    ss = _split_dot(x * x, seg)
    return x * lax.rsqrt(ss * (1.0 / HEAD_DIM) + EPS) * g


def _log2_sigmoids(y):
    log_b = jnp.minimum(y, 0.0) - jnp.log2(1.0 + jnp.exp2(-jnp.abs(y)))
    return log_b, log_b - y


def _stick_weights(y, r, tmat, causal):
    log_b, log_1mb = _log2_sigmoids(y)
    n_blocks = y.shape[1] // KEY_BLOCK
    weights = [None] * n_blocks
    for i in range(n_blocks - 1, -1, -1):
        cols = slice(i * KEY_BLOCK, (i + 1) * KEY_BLOCK)
        masked = causal is not None and i == n_blocks - 1
        l1 = jnp.where(causal, log_1mb[:, cols], 0.0) if masked else log_1mb[:, cols]
        between = _dot(l1.astype(BF16), tmat) + r
        a = jnp.exp2(log_b[:, cols] + between)
        weights[i] = (jnp.where(causal, a, 0.0) if masked else a).astype(BF16)
        r = r + jnp.sum(l1, axis=-1, keepdims=True)
    return (weights[0] if n_blocks == 1 else jnp.concatenate(weights, axis=1)), r


def _suffix_ones(n):
    later = lax.broadcasted_iota(jnp.int32, (n, n), 0) > lax.broadcasted_iota(jnp.int32, (n, n), 1)
    return jnp.where(later, 1.0, 0.0).astype(BF16), later


def _ffn(x, shift, scale, gate, g_in, g_out, wgu_ref, wdn_ref, act_ref):
    d_ff = wdn_ref.shape[0]
    h = (_rms(x, g_in) * (1.0 + scale) + shift).astype(BF16)
    for c0 in range(0, d_ff, 2 * MXU_DIM):
        c1 = min(c0 + 2 * MXU_DIM, d_ff)
        g = _dot(h, wgu_ref[:, c0:c1])
        u = _dot(h, wgu_ref[:, d_ff + c0:d_ff + c1])
        act_ref[:, c0:c1] = (g * _sigmoid(g) * u).astype(BF16)
    r = _dot(act_ref[...], wdn_ref[...])
    return x + FFN_RES * gate * _rms(r, g_out)


def _mod_kernel(c_ref, w_ref, b_ref, o_ref):
    c = c_ref[...]
    o_ref[...] = _dot((c * _sigmoid(c)).astype(BF16), w_ref[...]) + b_ref[...]


def _mod_call(c_all, w_c, b_c):
    n, d = c_all.shape
    width = w_c.shape[1]
    return pl.pallas_call(
        _mod_kernel,
        grid=(width // d,),
        in_specs=[pl.BlockSpec((n, d), lambda j: (0, 0)),
                  pl.BlockSpec((d, d), lambda j: (0, j)),
                  pl.BlockSpec((1, d), lambda j: (0, j))],
        out_specs=pl.BlockSpec((n, d), lambda j: (0, j)),
        out_shape=jax.ShapeDtypeStruct((n, width), F32),
        name="mod",
    )(c_all, w_c, b_c.reshape(1, width))


def _ffn_mix_in_kernel(sample, x_ref, sh0_ref, sc0_ref, ga0_ref, sh1_ref, sc1_ref,
                       g0_ref, g1_ref, g2_ref, wgu_ref, wdn_ref, win_ref,
                       vg_ref, og_ref, seg_ref, ws_ref, bs_ref, *rest):
    if sample:
        x1_ref, k_ref, v_ref, qb_ref, gv_ref, mgn_ref, act_ref = rest
    else:
        x1_ref, kt_ref, vt_ref, qb_ref, ktb_ref, vb_ref, mgn_ref, act_ref, mg_ref = rest
    tm = x_ref.shape[0]
    sbw = seg_ref.shape[0]

    x1 = _ffn(x_ref[...], sh0_ref[...], sc0_ref[...], ga0_ref[...], g0_ref[...], g1_ref[...],
              wgu_ref, wdn_ref, act_ref)
    x1_ref[...] = x1

    h = (_rms(x1, g2_ref[...]) * (1.0 + sc1_ref[...]) + sh1_ref[...]).astype(BF16)
    qb_ref[...] = (_dot(h, win_ref[:, 0:sbw]) * (HEAD_DIM ** -0.5 * LOG2E)).astype(BF16)
    k = _dot(h, win_ref[:, sbw:2 * sbw])
    v = _dot(h, win_ref[:, 2 * sbw:3 * sbw])
    u = _dot(h, win_ref[:, 3 * sbw:4 * sbw])
    gv = _head_rms(_dot(h, win_ref[:, 4 * sbw:5 * sbw]), vg_ref[...], seg_ref[...])

    if sample:
        k_ref[...] = k
        v_ref[...] = v
        gv_ref[...] = gv
        mg = u * (gv * ws_ref[...] + bs_ref[...])
    else:
        kt = k.T
        kt_ref[...] = kt
        ktb_ref[...] = kt.astype(BF16)
        vt_ref[...] = v.T
        vb_ref[...] = v.astype(BF16)
        even_head = (lax.broadcasted_iota(jnp.int32, (1, sbw), 1) // HEAD_DIM) % 2 == 0
        gv_even = jnp.where(even_head, gv, 0.0).astype(BF16)
        gv_odd = jnp.where(even_head, 0.0, gv).astype(BF16)
        row = lax.broadcasted_iota(jnp.int32, (CHUNK, CHUNK), 0)
        col = lax.broadcasted_iota(jnp.int32, (CHUNK, CHUNK), 1)
        tril = row >= col
        for p in range(sbw // LANES):
            wp = jnp.concatenate([jnp.where(tril, ws_ref[2 * p], 0.0),
                                  jnp.where(tril, ws_ref[2 * p + 1], 0.0)], axis=1).astype(BF16)
            cols = slice(p * LANES, (p + 1) * LANES)
            for r0 in range(0, tm, CHUNK):
                rows = slice(r0, r0 + CHUNK)
                stacked = jnp.concatenate([gv_even[rows, cols], gv_odd[rows, cols]], axis=0)
                mixed = _dot(wp, stacked) + bs_ref[:, cols]
                mg_ref[rows, cols] = u[rows, cols] * mixed
        mg = mg_ref[...]
    mgn_ref[...] = _head_rms(mg, og_ref[...], seg_ref[...]).astype(BF16)


def _resident(shape):
    nd = len(shape)
    return pl.BlockSpec(shape, lambda i: (0,) * nd, pipeline_mode=pl.Buffered(1))


def _mod_specs(mod, cols, tm, rows_per_mod):
    d = mod.shape[-1] // (3 * N_SUB)
    if mod.ndim == 3:
        tiles = rows_per_mod // tm
        return [pl.BlockSpec((None, 1, d), functools.partial(lambda i, c: (i // tiles, 0, c), c=c))
                for c in cols]
    return [pl.BlockSpec((tm, d), functools.partial(lambda i, c: (i, c), c=c)) for c in cols]


def _norm_specs(idx, d):
    return [pl.BlockSpec((None, 1, d), functools.partial(lambda i, c: (c, 0, 0), c=c),
                         pipeline_mode=pl.Buffered(1)) for c in idx]


def _ffn_mix_in_call(sample, x, mod, rows_per_mod, norm_g3, wgu, wdn, win, vg, og, seg, ws, bs, tm):
    n, d = x.shape
    d_ff = wdn.shape[0]
    sbw = seg.shape[0]
    row = lambda w: pl.BlockSpec((tm, w), lambda i: (i, 0))
    in_specs = ([row(d)] + _mod_specs(mod, (0, 1, 2, 3, 4), tm, rows_per_mod)
                + _norm_specs((0, 1, 2), d)
                + [_resident(a.shape) for a in (wgu, wdn, win, vg, og, seg, ws, bs)])
    f32_out = jax.ShapeDtypeStruct((n, sbw), F32)
    bf_out = jax.ShapeDtypeStruct((n, sbw), BF16)
    if sample:
        out_shape = [jax.ShapeDtypeStruct((n, d), F32), f32_out, f32_out, bf_out, f32_out, bf_out]
        out_specs = [row(s.shape[1]) for s in out_shape]
        scratch = [pltpu.VMEM((tm, d_ff), BF16)]
    else:
        tiles = rows_per_mod // tm
        n_seq = n // rows_per_mod
        t_out = lambda dt: jax.ShapeDtypeStruct((n_seq, sbw, rows_per_mod), dt)
        t_spec = pl.BlockSpec((None, sbw, tm), lambda i: (i // tiles, 0, i % tiles))
        out_shape = [jax.ShapeDtypeStruct((n, d), F32), t_out(F32), t_out(F32), bf_out, t_out(BF16),
                     bf_out, bf_out]
        out_specs = [row(d), t_spec, t_spec, row(sbw), t_spec, row(sbw), row(sbw)]
        scratch = [pltpu.VMEM((tm, d_ff), BF16), pltpu.VMEM((tm, sbw), F32)]
    return pl.pallas_call(
        functools.partial(_ffn_mix_in_kernel, sample),
        grid=(n // tm,),
        in_specs=in_specs,
        out_specs=out_specs,
        out_shape=out_shape,
        scratch_shapes=scratch,
        compiler_params=pltpu.CompilerParams(dimension_semantics=("arbitrary",),
                                             vmem_limit_bytes=VMEM_LIMIT_BYTES),
        name="ffn_mix_in_sample" if sample else "ffn_mix_in",
    )(x, mod, mod, mod, mod, mod, norm_g3, norm_g3, norm_g3, wgu, wdn, win, vg, og, seg, ws, bs)


def _sb_prompt_kernel(bias_ref, q_ref, kt_ref, v_ref, g_ref, o_ref, acc_ref, r_ref):
    seq = q_ref.shape[0]
    t, h = ATTN_TILE, KEY_BLOCK
    pair = pl.program_id(1)
    lane = lax.broadcasted_iota(jnp.int32, (1, LANES), 1)
    keep = (jnp.where(lane < HEAD_DIM, 1.0, 0.0).astype(BF16),
            jnp.where(lane < HEAD_DIM, 0.0, 1.0).astype(BF16))
    tmat, causal = _suffix_ones(h)
    si = lax.broadcasted_iota(jnp.int32, (LANES, LANES), 0) // HEAD_DIM
    sj = lax.broadcasted_iota(jnp.int32, (LANES, LANES), 1) // HEAD_DIM
    seg = jnp.where(si == sj, 1.0, 0.0).astype(BF16)
    biases = (bias_ref[2 * pair] * LOG2E, bias_ref[2 * pair + 1] * LOG2E)
    halves = (slice(0, h), slice(h, t))

    def q_tile(qi, carry):
        r0 = pl.multiple_of(qi * t, t)
        q = q_ref[pl.ds(r0, t), :]
        qms = (q * keep[0], q * keep[1])

        kt = kt_ref[:, pl.ds(r0, t)]
        v = v_ref[pl.ds(r0, t), :]
        acc = [None, None]
        for hh in range(2):
            vm = v * keep[hh]
            for half, keys in ((0, halves[0]), (1, slice(0, t))):
                y = _dot(qms[hh][halves[half]], kt[:, keys]) + biases[hh]
                w, r = _stick_weights(y, 0.0, tmat, causal)
                pv = _dot(w, vm[keys])
                acc[half] = pv if acc[half] is None else acc[half] + pv
                r_ref[hh, halves[half], :] = r
        acc_ref[halves[0], :] = acc[0]
        acc_ref[halves[1], :] = acc[1]

        def k_tile(j, c):
            k0 = pl.multiple_of((qi - 1 - j) * t, t)
            kt = kt_ref[:, pl.ds(k0, t)]
            v = v_ref[pl.ds(k0, t), :]
            acc = [acc_ref[halves[0], :], acc_ref[halves[1], :]]
            for hh in range(2):
                vm = v * keep[hh]
                for half in range(2):
                    y = _dot(qms[hh][halves[half]], kt) + biases[hh]
                    w, r = _stick_weights(y, r_ref[hh, halves[half], :], tmat, None)
                    acc[half] = acc[half] + _dot(w, vm)
                    r_ref[hh, halves[half], :] = r
            acc_ref[halves[0], :] = acc[0]
            acc_ref[halves[1], :] = acc[1]
            return c

        lax.fori_loop(0, qi, k_tile, 0)
        o_ref[pl.ds(r0, t), :] = _head_rms(acc_ref[...], g_ref[...], seg).astype(BF16)
        return carry

    lax.fori_loop(0, seq // t, q_tile, 0)


def _sb_prompt_call(qb, ktb, vb, bias, g, seq):
    n, sbw = qb.shape
    blk = lambda: pl.BlockSpec((seq, LANES), lambda b, p: (b, p))
    return pl.pallas_call(
        _sb_prompt_kernel,
        grid=(n // seq, sbw // LANES),
        in_specs=[pl.BlockSpec(memory_space=pltpu.SMEM), blk(),
                  pl.BlockSpec((None, LANES, seq), lambda b, p: (b, p, 0)), blk(),
                  pl.BlockSpec((1, LANES), lambda b, p: (0, p))],
        out_specs=blk(),
        out_shape=jax.ShapeDtypeStruct((n, sbw), BF16),
        scratch_shapes=[pltpu.VMEM((ATTN_TILE, LANES), F32), pltpu.VMEM((2, ATTN_TILE, 1), F32)],
        compiler_params=pltpu.CompilerParams(dimension_semantics=("arbitrary", "arbitrary"),
                                             vmem_limit_bytes=VMEM_LIMIT_BYTES),
        name="sb_prompt",
    )(bias, qb, ktb, vb, g)


def _sb_sample_kernel(pt_ref, bias_ref, q_ref, g_ref, seg_ref, *rest):
    pg = PAGES_PER_STEP
    k_refs, v_refs = rest[:pg], rest[pg:2 * pg]
    o_ref, acc_ref, r_ref = rest[2 * pg:]
    step = pl.program_id(1)
    n_heads, sbw = acc_ref.shape
    h = KEY_BLOCK
    n_blocks = pg * PAGE_SIZE // h

    @pl.when(step == 0)
    def _():
        acc_ref[...] = jnp.zeros_like(acc_ref)
        r_ref[...] = jnp.zeros_like(r_ref)

    head_of_row = lax.broadcasted_iota(jnp.int32, (n_heads, sbw), 0)
    head_of_col = lax.broadcasted_iota(jnp.int32, (n_heads, sbw), 1) // HEAD_DIM
    own = head_of_row == head_of_col
    q = jnp.broadcast_to(q_ref[...].astype(F32), (n_heads, sbw))
    qm = jnp.where(own, q, 0.0).astype(BF16)
    tmat, _ = _suffix_ones(h)

    keys = jnp.concatenate([k_refs[i][...].astype(BF16) for i in range(pg)], axis=1)
    y = _dot(qm, keys) + bias_ref[...] * LOG2E
    log_b, log_1mb = _log2_sigmoids(y)
    stacked = jnp.concatenate([log_1mb[:, i * h:(i + 1) * h] for i in range(n_blocks)], axis=0)
    within = _dot(stacked.astype(BF16), tmat)
    totals = jnp.sum(stacked, axis=-1, keepdims=True)
    r = r_ref[...]
    between = [None] * n_blocks
    for i in range(n_blocks - 1, -1, -1):
        rows = slice(i * n_heads, (i + 1) * n_heads)
        between[i] = within[rows] + r
        r = r + totals[rows]
    a = jnp.exp2(log_b + jnp.concatenate(between, axis=1)).astype(BF16)
    vals = jnp.concatenate([v_refs[i][...].astype(BF16) for i in range(pg)], axis=1)
    acc = acc_ref[...] + _dot_nt(a, vals)
    acc_ref[...] = acc
    r_ref[...] = r

    @pl.when(step == pl.num_programs(1) - 1)
    def _():
        o = jnp.sum(jnp.where(own, acc, 0.0), axis=0, keepdims=True)
        o_ref[...] = _head_rms(o, g_ref[...], seg_ref[...]).astype(BF16)


def _sb_sample_call(page_table, bias, qb, g, seg, pool_k, pool_v):
    nb, n_pages = page_table.shape
    sbw = qb.shape[1]
    n_heads = sbw // HEAD_DIM
    pg = PAGES_PER_STEP
    n_steps = n_pages // pg
    pool_k = pool_k.transpose(0, 2, 3, 1).reshape(pool_k.shape[0], sbw, PAGE_SIZE)
    pool_v = pool_v.transpose(0, 2, 3, 1).reshape(pool_v.shape[0], sbw, PAGE_SIZE)

    def page_spec(i):
        def index(b, s, pt):
            return (pt[b * n_pages + (n_steps - 1 - s) * pg + i], 0, 0)
        return pl.BlockSpec((None, sbw, PAGE_SIZE), index)

    whole = lambda shape: pl.BlockSpec(shape, lambda b, s, pt: (0,) * len(shape))
    grid_spec = pltpu.PrefetchScalarGridSpec(
        num_scalar_prefetch=1,
        grid=(nb, n_steps),
        in_specs=([whole((n_heads, 1)),
                   pl.BlockSpec((None, 1, sbw), lambda b, s, pt: (b, 0, 0)),
                   whole((1, sbw)), whole((sbw, sbw))]
                  + [page_spec(i) for i in range(pg)] * 2),
        out_specs=pl.BlockSpec((None, 1, sbw), lambda b, s, pt: (b, 0, 0)),
        scratch_shapes=[pltpu.VMEM((n_heads, sbw), F32), pltpu.VMEM((n_heads, 1), F32)],
    )
    out = pl.pallas_call(
        _sb_sample_kernel,
        grid_spec=grid_spec,
        out_shape=jax.ShapeDtypeStruct((nb, 1, sbw), BF16),
        compiler_params=pltpu.CompilerParams(dimension_semantics=("arbitrary", "arbitrary"),
                                             vmem_limit_bytes=VMEM_LIMIT_BYTES),
        name="sb_sample",
    )(page_table.reshape(-1), bias.reshape(n_heads, 1), qb.reshape(nb, 1, sbw), g, seg,
      *([pool_k] * pg), *([pool_v] * pg))
    return out.reshape(nb, sbw)


def _mix_out_ffn_kernel(x_ref, sbn_ref, mgn_ref, ga1_ref, sh2_ref, sc2_ref, ga2_ref,
                        g3_ref, g4_ref, g5_ref, wout_ref, wgu_ref, wdn_ref, y_ref, act_ref):
    sbw = sbn_ref.shape[1]
    o = _dot(sbn_ref[...], wout_ref[0:sbw, :]) + _dot(mgn_ref[...], wout_ref[sbw:, :])
    x2 = x_ref[...] + ga1_ref[...] * _rms(o, g3_ref[...])
    y_ref[...] = _ffn(x2, sh2_ref[...], sc2_ref[...], ga2_ref[...], g4_ref[...], g5_ref[...],
                      wgu_ref, wdn_ref, act_ref)


def _mix_out_ffn_call(sample, x, sbn, mgn, mod, rows_per_mod, norm_g3, wout, wgu, wdn, tm):
    n, d = x.shape
    d_ff = wdn.shape[0]
    row = lambda w: pl.BlockSpec((tm, w), lambda i: (i, 0))
    in_specs = ([row(d), row(sbn.shape[1]), row(mgn.shape[1])]
                + _mod_specs(mod, (5, 6, 7, 8), tm, rows_per_mod)
                + _norm_specs((3, 4, 5), d)
                + [_resident(a.shape) for a in (wout, wgu, wdn)])
    return pl.pallas_call(
        _mix_out_ffn_kernel,
        grid=(n // tm,),
        in_specs=in_specs,
        out_specs=row(d),
        out_shape=jax.ShapeDtypeStruct((n, d), F32),
        scratch_shapes=[pltpu.VMEM((tm, d_ff), BF16)],
        compiler_params=pltpu.CompilerParams(dimension_semantics=("arbitrary",),
                                             vmem_limit_bytes=VMEM_LIMIT_BYTES),
        name="mix_out_ffn_sample" if sample else "mix_out_ffn",
    )(x, sbn, mgn, mod, mod, mod, mod, norm_g3, norm_g3, norm_g3, wout, wgu, wdn)


def kernel(x_prompt, x_sample, c_prompt, c_sample, cache_k, cache_v, page_table, w_c, b_c, norm_g, w_in, sb_bias, mlp_v_g, w_s, b_s, sb_out_g, mlp_out_g, w_out, w1_gu, w1_dn, w2_gu, w2_dn):
    depth = w_c.shape[0]
    nb_p, seq, d = x_prompt.shape
    nb_s, dec_seq, _ = x_sample.shape
    n_sb, hd = sb_out_g.shape[1:]
    n_mlp = mlp_v_g.shape[1]
    sbw = n_sb * hd
    assert hd == HEAD_DIM and n_mlp * hd == sbw and dec_seq == 1
    assert w_s.shape[2] == CHUNK and cache_k.shape[2] == PAGE_SIZE
    assert seq % ATTN_TILE == 0 and seq % ROW_TILE == 0 and ROW_TILE % CHUNK == 0
    assert page_table.shape[1] % PAGES_PER_STEP == 0

    head_id = jnp.arange(sbw) // HEAD_DIM
    seg = (head_id[:, None] == head_id[None, :]).astype(BF16)
    flat = lambda a: a.reshape(1, sbw)

    xp = x_prompt.reshape(nb_p * seq, d)
    xs = x_sample.reshape(nb_s, d)
    c_all = jnp.concatenate([c_prompt, c_sample], axis=0)
    outs = [[] for _ in range(5)]
    for l in range(depth):
        mod = _mod_call(c_all, w_c[l].astype(BF16), b_c[l])
        mod_p = mod[:nb_p].reshape(nb_p, 1, -1)
        mod_s = mod[nb_p:]
        norm_g3 = norm_g[l].reshape(2 * N_SUB, 1, d)
        w1 = (w1_gu[l].astype(BF16), w1_dn[l].astype(BF16))
        w2 = (w2_gu[l].astype(BF16), w2_dn[l].astype(BF16))
        win = w_in[l].astype(BF16)
        wout = w_out[l].astype(BF16)
        vg, og, sg = flat(mlp_v_g[l]), flat(mlp_out_g[l]), flat(sb_out_g[l])
        bs_rows = jnp.repeat(b_s[l].T, HEAD_DIM, axis=1)
        ws0 = flat(jnp.repeat(w_s[l][:, 0, 0], HEAD_DIM))
        bs0 = bs_rows[0:1]

        x1p, ktp, vtp, qbp, ktbp, vbp, mgnp = _ffn_mix_in_call(
            False, xp, mod_p, seq, norm_g3, *w1, win, vg, og, seg, w_s[l], bs_rows, ROW_TILE)
        x1s, ks, vs, qbs, gvs, mgns = _ffn_mix_in_call(
            True, xs, mod_s, 1, norm_g3, *w1, win, vg, og, seg, ws0, bs0, nb_s)

        sbnp = _sb_prompt_call(qbp, ktbp, vbp, sb_bias[l], sg, seq)
        sbns = _sb_sample_call(page_table, sb_bias[l], qbs, sg, seg, cache_k[l], cache_v[l])

        xp = _mix_out_ffn_call(False, x1p, sbnp, mgnp, mod_p, seq, norm_g3, wout, *w2, ROW_TILE)
        xs = _mix_out_ffn_call(True, x1s, sbns, mgns, mod_s, 1, norm_g3, wout, *w2, nb_s)

        outs[0].append(ktp.reshape(nb_p, n_sb, hd, seq).transpose(0, 3, 1, 2))
        outs[1].append(vtp.reshape(nb_p, n_sb, hd, seq).transpose(0, 3, 1, 2))
        outs[2].append(ks.reshape(nb_s, 1, n_sb, hd))
        outs[3].append(vs.reshape(nb_s, 1, n_sb, hd))
        outs[4].append(gvs.reshape(nb_s, 1, n_mlp, hd))
    return (xp.reshape(nb_p, seq, d), xs.reshape(nb_s, 1, d)) + tuple(jnp.stack(o) for o in outs)
```

```python
import functools

import jax
import jax.numpy as jnp
from jax import lax
from jax.experimental import pallas as pl
from jax.experimental.pallas import tpu as pltpu

F32 = jnp.float32
BF16 = jnp.bfloat16

HEAD_DIM = 64
CHUNK = 128
PAGE_SIZE = 128
N_SUB = 3
FFN_RES = 0.5
EPS = 1e-6

LANES = 128
MXU_DIM = 256
VMEM_LIMIT_BYTES = 56 * 1024 * 1024

ROW_TILE = 512
KEY_BLOCK = MXU_DIM
ATTN_TILE = 2 * KEY_BLOCK
PAGES_PER_STEP = 32
LOG2E = 1.4426950408889634


def _dot(a, b):
    return jnp.dot(a, b, preferred_element_type=F32)


def _dot_nt(a, b):
    return lax.dot_general(a, b, (((1,), (1,)), ((), ())), preferred_element_type=F32)


def _split_dot(x, m):
    hi = x.astype(BF16)
    lo = (x - hi.astype(F32)).astype(BF16)
    return _dot(hi, m) + _dot(lo, m)


def _sigmoid(x):
    return 1.0 / (1.0 + jnp.exp(-x))


def _rms(x, g):
    ms = jnp.mean(x * x, axis=-1, keepdims=True)
    return x * lax.rsqrt(ms + EPS) * g


def _head_rms(x, g, seg):
    ss = _split_dot(x * x, seg)
    return x * lax.rsqrt(ss * (1.0 / HEAD_DIM) + EPS) * g


def _log2_sigmoids(y):
    log_b = jnp.minimum(y, 0.0) - jnp.log2(1.0 + jnp.exp2(-jnp.abs(y)))
    return log_b, log_b - y


def _suffix_ones(n):
    later = lax.broadcasted_iota(jnp.int32, (n, n), 0) > lax.broadcasted_iota(jnp.int32, (n, n), 1)
    return jnp.where(later, 1.0, 0.0).astype(BF16), later


def _ffn(x, shift, scale, gate, g_in, g_out, wgu_ref, wdn_ref, act_ref):
    d_ff = wdn_ref.shape[0]
    h = (_rms(x, g_in) * (1.0 + scale) + shift).astype(BF16)
    for c0 in range(0, d_ff, 2 * MXU_DIM):
        c1 = min(c0 + 2 * MXU_DIM, d_ff)
        g = _dot(h, wgu_ref[:, c0:c1])
        u = _dot(h, wgu_ref[:, d_ff + c0:d_ff + c1])
        act_ref[:, c0:c1] = (g * _sigmoid(g) * u).astype(BF16)
    r = _dot(act_ref[...], wdn_ref[...])
    return x + FFN_RES * gate * _rms(r, g_out)


def _mod_kernel(c_ref, w_ref, b_ref, o_ref):
    c = c_ref[...]
    o_ref[...] = _dot((c * _sigmoid(c)).astype(BF16), w_ref[...]) + b_ref[...]


def _mod_call(c_all, w_c, b_c):
    n, d = c_all.shape
    width = w_c.shape[1]
    return pl.pallas_call(
        _mod_kernel,
        grid=(width // d,),
        in_specs=[pl.BlockSpec((n, d), lambda j: (0, 0)),
                  pl.BlockSpec((d, d), lambda j: (0, j)),
                  pl.BlockSpec((1, d), lambda j: (0, j))],
        out_specs=pl.BlockSpec((n, d), lambda j: (0, j)),
        out_shape=jax.ShapeDtypeStruct((n, width), F32),
        name="mod",
    )(c_all, w_c, b_c.reshape(1, width))


def _ffn_mix_in_kernel(sample, x_ref, sh0_ref, sc0_ref, ga0_ref, sh1_ref, sc1_ref,
                       g0_ref, g1_ref, g2_ref, wgu_ref, wdn_ref, win_ref,
                       vg_ref, og_ref, seg_ref, ws_ref, bs_ref, *rest):
    if sample:
        x1_ref, k_ref, v_ref, qb_ref, gv_ref, mgn_ref, act_ref = rest
    else:
        x1_ref, kt_ref, vt_ref, qb_ref, ktb_ref, vb_ref, mgn_ref, act_ref, mg_ref = rest
    tm = x_ref.shape[0]
    sbw = seg_ref.shape[0]

    x1 = _ffn(x_ref[...], sh0_ref[...], sc0_ref[...], ga0_ref[...], g0_ref[...], g1_ref[...],
              wgu_ref, wdn_ref, act_ref)
    x1_ref[...] = x1

    h = (_rms(x1, g2_ref[...]) * (1.0 + sc1_ref[...]) + sh1_ref[...]).astype(BF16)
    qb_ref[...] = (_dot(h, win_ref[:, 0:sbw]) * (HEAD_DIM ** -0.5 * LOG2E)).astype(BF16)
    k = _dot(h, win_ref[:, sbw:2 * sbw])
    v = _dot(h, win_ref[:, 2 * sbw:3 * sbw])
    u = _dot(h, win_ref[:, 3 * sbw:4 * sbw])
    gv = _head_rms(_dot(h, win_ref[:, 4 * sbw:5 * sbw]), vg_ref[...], seg_ref[...])

    if sample:
        k_ref[...] = k
        v_ref[...] = v
        gv_ref[...] = gv
        mg = u * (gv * ws_ref[...] + bs_ref[...])
    else:
        kt = k.T
        kt_ref[...] = kt
        ktb_ref[...] = kt.astype(BF16)
        vt_ref[...] = v.T
        vb_ref[...] = v.astype(BF16)
        even_head = (lax.broadcasted_iota(jnp.int32, (1, sbw), 1) // HEAD_DIM) % 2 == 0
        gv_even = jnp.where(even_head, gv, 0.0).astype(BF16)
        gv_odd = jnp.where(even_head, 0.0, gv).astype(BF16)
        row = lax.broadcasted_iota(jnp.int32, (CHUNK, CHUNK), 0)
        col = lax.broadcasted_iota(jnp.int32, (CHUNK, CHUNK), 1)
        tril = row >= col
        for p in range(sbw // LANES):
            wp = jnp.concatenate([jnp.where(tril, ws_ref[2 * p], 0.0),
                                  jnp.where(tril, ws_ref[2 * p + 1], 0.0)], axis=1).astype(BF16)
            cols = slice(p * LANES, (p + 1) * LANES)
            for r0 in range(0, tm, CHUNK):
                rows = slice(r0, r0 + CHUNK)
                stacked = jnp.concatenate([gv_even[rows, cols], gv_odd[rows, cols]], axis=0)
                mixed = _dot(wp, stacked) + bs_ref[:, cols]
                mg_ref[rows, cols] = u[rows, cols] * mixed
        mg = mg_ref[...]
    mgn_ref[...] = _head_rms(mg, og_ref[...], seg_ref[...]).astype(BF16)


def _resident(shape):
    nd = len(shape)
    return pl.BlockSpec(shape, lambda i: (0,) * nd, pipeline_mode=pl.Buffered(1))


def _mod_specs(mod, cols, tm, rows_per_mod):
    d = mod.shape[-1] // (3 * N_SUB)
    if mod.ndim == 3:
        tiles = rows_per_mod // tm
        return [pl.BlockSpec((None, 1, d), functools.partial(lambda i, c: (i // tiles, 0, c), c=c))
                for c in cols]
    return [pl.BlockSpec((tm, d), functools.partial(lambda i, c: (i, c), c=c)) for c in cols]


def _norm_specs(idx, d):
    return [pl.BlockSpec((None, 1, d), functools.partial(lambda i, c: (c, 0, 0), c=c),
                         pipeline_mode=pl.Buffered(1)) for c in idx]


def _ffn_mix_in_call(sample, x, mod, rows_per_mod, norm_g3, wgu, wdn, win, vg, og, seg, ws, bs, tm):
    n, d = x.shape
    d_ff = wdn.shape[0]
    sbw = seg.shape[0]
    row = lambda w: pl.BlockSpec((tm, w), lambda i: (i, 0))
    in_specs = ([row(d)] + _mod_specs(mod, (0, 1, 2, 3, 4), tm, rows_per_mod)
                + _norm_specs((0, 1, 2), d)
                + [_resident(a.shape) for a in (wgu, wdn, win, vg, og, seg, ws, bs)])
    f32_out = jax.ShapeDtypeStruct((n, sbw), F32)
    bf_out = jax.ShapeDtypeStruct((n, sbw), BF16)
    if sample:
        out_shape = [jax.ShapeDtypeStruct((n, d), F32), f32_out, f32_out, bf_out, f32_out, bf_out]
        out_specs = [row(s.shape[1]) for s in out_shape]
        scratch = [pltpu.VMEM((tm, d_ff), BF16)]
    else:
        tiles = rows_per_mod // tm
        n_seq = n // rows_per_mod
        t_out = lambda dt: jax.ShapeDtypeStruct((n_seq, sbw, rows_per_mod), dt)
        t_spec = pl.BlockSpec((None, sbw, tm), lambda i: (i // tiles, 0, i % tiles))
        out_shape = [jax.ShapeDtypeStruct((n, d), F32), t_out(F32), t_out(F32), bf_out, t_out(BF16),
                     bf_out, bf_out]
        out_specs = [row(d), t_spec, t_spec, row(sbw), t_spec, row(sbw), row(sbw)]
        scratch = [pltpu.VMEM((tm, d_ff), BF16), pltpu.VMEM((tm, sbw), F32)]
    return pl.pallas_call(
        functools.partial(_ffn_mix_in_kernel, sample),
        grid=(n // tm,),
        in_specs=in_specs,
        out_specs=out_specs,
        out_shape=out_shape,
        scratch_shapes=scratch,
        compiler_params=pltpu.CompilerParams(dimension_semantics=("arbitrary",),
                                             vmem_limit_bytes=VMEM_LIMIT_BYTES),
        name="ffn_mix_in_sample" if sample else "ffn_mix_in",
    )(x, mod, mod, mod, mod, mod, norm_g3, norm_g3, norm_g3, wgu, wdn, win, vg, og, seg, ws, bs)


def _sb_prompt_kernel(bias_ref, q_ref, kt_ref, v_ref, g_ref, o_ref,
                      acc_ref, r_ref, lb_ref, l1_ref, tot_ref):
    seq = q_ref.shape[0]
    t, h = ATTN_TILE, KEY_BLOCK
    pair = pl.program_id(1)
    lane = lax.broadcasted_iota(jnp.int32, (1, LANES), 1)
    keep = (jnp.where(lane < HEAD_DIM, 1.0, 0.0).astype(BF16),
            jnp.where(lane < HEAD_DIM, 0.0, 1.0).astype(BF16))
    tmat, causal = _suffix_ones(h)
    si = lax.broadcasted_iota(jnp.int32, (LANES, LANES), 0) // HEAD_DIM
    sj = lax.broadcasted_iota(jnp.int32, (LANES, LANES), 1) // HEAD_DIM
    seg = jnp.where(si == sj, 1.0, 0.0).astype(BF16)
    biases = (bias_ref[2 * pair] * LOG2E, bias_ref[2 * pair + 1] * LOG2E)
    halves = (slice(0, h), slice(h, t))

    def n_blocks(diag, half):
        return 1 if diag and half == 0 else t // h

    def scores(slot, qm, kt, bias, diag, hh, half):
        nb = n_blocks(diag, half)
        y = _dot(qm[halves[half]], kt[:, :nb * h]) + bias
        log_b, log_1mb = _log2_sigmoids(y)
        for blk in range(nb):
            cols = slice(blk * h, (blk + 1) * h)
            l1 = log_1mb[:, cols]
            if diag and blk == nb - 1:
                l1 = jnp.where(causal, l1, 0.0)
            lb_ref[slot, hh, half, :, cols] = log_b[:, cols]
            l1_ref[slot, hh, half, :, cols] = l1.astype(BF16)
            tot_ref[slot, hh, half, blk] = jnp.sum(l1, axis=-1, keepdims=True)

    def values(slot, vm, diag, hh, half):
        nb = n_blocks(diag, half)
        r = 0.0 if diag else r_ref[hh, halves[half], :]
        weights = [None] * nb
        for blk in range(nb - 1, -1, -1):
            cols = slice(blk * h, (blk + 1) * h)
            between = _dot(l1_ref[slot, hh, half, :, cols], tmat) + r
            a = jnp.exp2(lb_ref[slot, hh, half, :, cols] + between)
            if diag and blk == nb - 1:
                a = jnp.where(causal, a, 0.0)
            weights[blk] = a.astype(BF16)
            r = r + tot_ref[slot, hh, half, blk]
        w = weights[0] if nb == 1 else jnp.concatenate(weights, axis=1)
        pv = _dot(w, vm[:nb * h])
        if diag and hh == 0:
            acc_ref[halves[half], :] = pv
        else:
            acc_ref[halves[half], :] += pv
        r_ref[hh, halves[half], :] = r

    chains = [(hh, half) for hh in range(2) for half in range(2)]
    for qi in range(seq // t):
        r0 = qi * t
        q = q_ref[r0:r0 + t, :]
        qms = (q * keep[0], q * keep[1])
        kt = kt_ref[:, r0:r0 + t]
        for hh, half in chains:
            scores(0, qms[hh], kt, biases[hh], True, hh, half)
        for k in range(qi + 1):
            k0 = r0 - k * t
            v = v_ref[k0:k0 + t, :]
            vms = (v * keep[0], v * keep[1])
            if k < qi:
                kt = kt_ref[:, k0 - t:k0]
            for hh, half in chains:
                values(k % 2, vms[hh], k == 0, hh, half)
                if k < qi:
                    scores((k + 1) % 2, qms[hh], kt, biases[hh], False, hh, half)
        o_ref[r0:r0 + t, :] = _head_rms(acc_ref[...], g_ref[...], seg).astype(BF16)


def _sb_prompt_call(qb, ktb, vb, bias, g, seq):
    n, sbw = qb.shape
    blk = lambda: pl.BlockSpec((seq, LANES), lambda b, p: (b, p))
    return pl.pallas_call(
        _sb_prompt_kernel,
        grid=(n // seq, sbw // LANES),
        in_specs=[pl.BlockSpec(memory_space=pltpu.SMEM), blk(),
                  pl.BlockSpec((None, LANES, seq), lambda b, p: (b, p, 0)), blk(),
                  pl.BlockSpec((1, LANES), lambda b, p: (0, p))],
        out_specs=blk(),
        out_shape=jax.ShapeDtypeStruct((n, sbw), BF16),
        scratch_shapes=[pltpu.VMEM((ATTN_TILE, LANES), F32), pltpu.VMEM((2, ATTN_TILE, 1), F32),
                        pltpu.VMEM((2, 2, 2, KEY_BLOCK, ATTN_TILE), F32),
                        pltpu.VMEM((2, 2, 2, KEY_BLOCK, ATTN_TILE), BF16),
                        pltpu.VMEM((2, 2, 2, ATTN_TILE // KEY_BLOCK, KEY_BLOCK, 1), F32)],
        compiler_params=pltpu.CompilerParams(dimension_semantics=("arbitrary", "arbitrary"),
                                             vmem_limit_bytes=VMEM_LIMIT_BYTES),
        name="sb_prompt",
    )(bias, qb, ktb, vb, g)


def _sb_sample_kernel(pt_ref, bias_ref, q_ref, g_ref, seg_ref, *rest):
    pg = PAGES_PER_STEP
    k_refs, v_refs = rest[:pg], rest[pg:2 * pg]
    o_ref, acc_ref, r_ref = rest[2 * pg:]
    step = pl.program_id(1)
    n_heads, sbw = acc_ref.shape
    h = KEY_BLOCK
    n_blocks = pg * PAGE_SIZE // h

    @pl.when(step == 0)
    def _():
        acc_ref[...] = jnp.zeros_like(acc_ref)
        r_ref[...] = jnp.zeros_like(r_ref)

    head_of_row = lax.broadcasted_iota(jnp.int32, (n_heads, sbw), 0)
    head_of_col = lax.broadcasted_iota(jnp.int32, (n_heads, sbw), 1) // HEAD_DIM
    own = head_of_row == head_of_col
    q = jnp.broadcast_to(q_ref[...].astype(F32), (n_heads, sbw))
    qm = jnp.where(own, q, 0.0).astype(BF16)
    tmat, _ = _suffix_ones(h)

    keys = jnp.concatenate([k_refs[i][...].astype(BF16) for i in range(pg)], axis=1)
    y = _dot(qm, keys) + bias_ref[...] * LOG2E
    log_b, log_1mb = _log2_sigmoids(y)
    stacked = jnp.concatenate([log_1mb[:, i * h:(i + 1) * h] for i in range(n_blocks)], axis=0)
    within = _dot(stacked.astype(BF16), tmat)
    totals = jnp.sum(stacked, axis=-1, keepdims=True)
    r = r_ref[...]
    between = [None] * n_blocks
    for i in range(n_blocks - 1, -1, -1):
        rows = slice(i * n_heads, (i + 1) * n_heads)
        between[i] = within[rows] + r
        r = r + totals[rows]
    a = jnp.exp2(log_b + jnp.concatenate(between, axis=1)).astype(BF16)
    vals = jnp.concatenate([v_refs[i][...].astype(BF16) for i in range(pg)], axis=1)
    acc = acc_ref[...] + _dot_nt(a, vals)
    acc_ref[...] = acc
    r_ref[...] = r

    @pl.when(step == pl.num_programs(1) - 1)
    def _():
        o = jnp.sum(jnp.where(own, acc, 0.0), axis=0, keepdims=True)
        o_ref[...] = _head_rms(o, g_ref[...], seg_ref[...]).astype(BF16)


def _sb_sample_call(page_table, bias, qb, g, seg, pool_k, pool_v):
    nb, n_pages = page_table.shape
    sbw = qb.shape[1]
    n_heads = sbw // HEAD_DIM
    pg = PAGES_PER_STEP
    n_steps = n_pages // pg
    pool_k = pool_k.transpose(0, 2, 3, 1).reshape(pool_k.shape[0], sbw, PAGE_SIZE)
    pool_v = pool_v.transpose(0, 2, 3, 1).reshape(pool_v.shape[0], sbw, PAGE_SIZE)

    def page_spec(i):
        def index(b, s, pt):
            return (pt[b * n_pages + (n_steps - 1 - s) * pg + i], 0, 0)
        return pl.BlockSpec((None, sbw, PAGE_SIZE), index)

    whole = lambda shape: pl.BlockSpec(shape, lambda b, s, pt: (0,) * len(shape))
    grid_spec = pltpu.PrefetchScalarGridSpec(
        num_scalar_prefetch=1,
        grid=(nb, n_steps),
        in_specs=([whole((n_heads, 1)),
                   pl.BlockSpec((None, 1, sbw), lambda b, s, pt: (b, 0, 0)),
                   whole((1, sbw)), whole((sbw, sbw))]
                  + [page_spec(i) for i in range(pg)] * 2),
        out_specs=pl.BlockSpec((None, 1, sbw), lambda b, s, pt: (b, 0, 0)),
        scratch_shapes=[pltpu.VMEM((n_heads, sbw), F32), pltpu.VMEM((n_heads, 1), F32)],
    )
    out = pl.pallas_call(
        _sb_sample_kernel,
        grid_spec=grid_spec,
        out_shape=jax.ShapeDtypeStruct((nb, 1, sbw), BF16),
        compiler_params=pltpu.CompilerParams(dimension_semantics=("arbitrary", "arbitrary"),
                                             vmem_limit_bytes=VMEM_LIMIT_BYTES),
        name="sb_sample",
    )(page_table.reshape(-1), bias.reshape(n_heads, 1), qb.reshape(nb, 1, sbw), g, seg,
      *([pool_k] * pg), *([pool_v] * pg))
    return out.reshape(nb, sbw)


def _mix_out_ffn_kernel(x_ref, sbn_ref, mgn_ref, ga1_ref, sh2_ref, sc2_ref, ga2_ref,
                        g3_ref, g4_ref, g5_ref, wout_ref, wgu_ref, wdn_ref, y_ref, act_ref):
    sbw = sbn_ref.shape[1]
    o = _dot(sbn_ref[...], wout_ref[0:sbw, :]) + _dot(mgn_ref[...], wout_ref[sbw:, :])
    x2 = x_ref[...] + ga1_ref[...] * _rms(o, g3_ref[...])
    y_ref[...] = _ffn(x2, sh2_ref[...], sc2_ref[...], ga2_ref[...], g4_ref[...], g5_ref[...],
                      wgu_ref, wdn_ref, act_ref)


def _mix_out_ffn_call(sample, x, sbn, mgn, mod, rows_per_mod, norm_g3, wout, wgu, wdn, tm):
    n, d = x.shape
    d_ff = wdn.shape[0]
    row = lambda w: pl.BlockSpec((tm, w), lambda i: (i, 0))
    in_specs = ([row(d), row(sbn.shape[1]), row(mgn.shape[1])]
                + _mod_specs(mod, (5, 6, 7, 8), tm, rows_per_mod)
                + _norm_specs((3, 4, 5), d)
                + [_resident(a.shape) for a in (wout, wgu, wdn)])
    return pl.pallas_call(
        _mix_out_ffn_kernel,
        grid=(n // tm,),
        in_specs=in_specs,
        out_specs=row(d),
        out_shape=jax.ShapeDtypeStruct((n, d), F32),
        scratch_shapes=[pltpu.VMEM((tm, d_ff), BF16)],
        compiler_params=pltpu.CompilerParams(dimension_semantics=("arbitrary",),
                                             vmem_limit_bytes=VMEM_LIMIT_BYTES),
        name="mix_out_ffn_sample" if sample else "mix_out_ffn",
    )(x, sbn, mgn, mod, mod, mod, mod, norm_g3, norm_g3, norm_g3, wout, wgu, wdn)


def kernel(x_prompt, x_sample, c_prompt, c_sample, cache_k, cache_v, page_table, w_c, b_c, norm_g, w_in, sb_bias, mlp_v_g, w_s, b_s, sb_out_g, mlp_out_g, w_out, w1_gu, w1_dn, w2_gu, w2_dn):
    depth = w_c.shape[0]
    nb_p, seq, d = x_prompt.shape
    nb_s, dec_seq, _ = x_sample.shape
    n_sb, hd = sb_out_g.shape[1:]
    n_mlp = mlp_v_g.shape[1]
    sbw = n_sb * hd
    assert hd == HEAD_DIM and n_mlp * hd == sbw and dec_seq == 1
    assert w_s.shape[2] == CHUNK and cache_k.shape[2] == PAGE_SIZE
    assert seq % ATTN_TILE == 0 and seq % ROW_TILE == 0 and ROW_TILE % CHUNK == 0
    assert page_table.shape[1] % PAGES_PER_STEP == 0

    head_id = jnp.arange(sbw) // HEAD_DIM
    seg = (head_id[:, None] == head_id[None, :]).astype(BF16)
    flat = lambda a: a.reshape(1, sbw)

    xp = x_prompt.reshape(nb_p * seq, d)
    xs = x_sample.reshape(nb_s, d)
    c_all = jnp.concatenate([c_prompt, c_sample], axis=0)
    outs = [[] for _ in range(5)]
    for l in range(depth):
        mod = _mod_call(c_all, w_c[l].astype(BF16), b_c[l])
        mod_p = mod[:nb_p].reshape(nb_p, 1, -1)
        mod_s = mod[nb_p:]
        norm_g3 = norm_g[l].reshape(2 * N_SUB, 1, d)
        w1 = (w1_gu[l].astype(BF16), w1_dn[l].astype(BF16))
        w2 = (w2_gu[l].astype(BF16), w2_dn[l].astype(BF16))
        win = w_in[l].astype(BF16)
        wout = w_out[l].astype(BF16)
        vg, og, sg = flat(mlp_v_g[l]), flat(mlp_out_g[l]), flat(sb_out_g[l])
        bs_rows = jnp.repeat(b_s[l].T, HEAD_DIM, axis=1)
        ws0 = flat(jnp.repeat(w_s[l][:, 0, 0], HEAD_DIM))
        bs0 = bs_rows[0:1]

        x1p, ktp, vtp, qbp, ktbp, vbp, mgnp = _ffn_mix_in_call(
            False, xp, mod_p, seq, norm_g3, *w1, win, vg, og, seg, w_s[l], bs_rows, ROW_TILE)
        x1s, ks, vs, qbs, gvs, mgns = _ffn_mix_in_call(
            True, xs, mod_s, 1, norm_g3, *w1, win, vg, og, seg, ws0, bs0, nb_s)

        sbnp = _sb_prompt_call(qbp, ktbp, vbp, sb_bias[l], sg, seq)
        sbns = _sb_sample_call(page_table, sb_bias[l], qbs, sg, seg, cache_k[l], cache_v[l])

        xp = _mix_out_ffn_call(False, x1p, sbnp, mgnp, mod_p, seq, norm_g3, wout, *w2, ROW_TILE)
        xs = _mix_out_ffn_call(True, x1s, sbns, mgns, mod_s, 1, norm_g3, wout, *w2, nb_s)

        outs[0].append(ktp.reshape(nb_p, n_sb, hd, seq).transpose(0, 3, 1, 2))
        outs[1].append(vtp.reshape(nb_p, n_sb, hd, seq).transpose(0, 3, 1, 2))
        outs[2].append(ks.reshape(nb_s, 1, n_sb, hd))
        outs[3].append(vs.reshape(nb_s, 1, n_sb, hd))
        outs[4].append(gvs.reshape(nb_s, 1, n_mlp, hd))
    return (xp.reshape(nb_p, seq, d), xs.reshape(nb_s, 1, d)) + tuple(jnp.stack(o) for o in outs)
```

```python
import functools

import jax
import jax.numpy as jnp
from jax import lax
from jax.experimental import pallas as pl
from jax.experimental.pallas import tpu as pltpu

F32 = jnp.float32
BF16 = jnp.bfloat16

HEAD_DIM = 64
CHUNK = 128
PAGE_SIZE = 128
N_SUB = 3
FFN_RES = 0.5
EPS = 1e-6

LANES = 128
MXU_DIM = 256
VMEM_LIMIT_BYTES = 56 * 1024 * 1024

ROW_TILE = 512
KEY_BLOCK = MXU_DIM
ATTN_TILE = 2 * KEY_BLOCK
LOG2E = 1.4426950408889634


def _dot(a, b):
    return jnp.dot(a, b, preferred_element_type=F32)


def _dot_nt(a, b):
    return lax.dot_general(a, b, (((1,), (1,)), ((), ())), preferred_element_type=F32)


def _split_dot(x, m):
    hi = x.astype(BF16)
    lo = (x - hi.astype(F32)).astype(BF16)
    return _dot(hi, m) + _dot(lo, m)


def _sigmoid(x):
    return 1.0 / (1.0 + jnp.exp(-x))


def _rms(x, g):
    ms = jnp.mean(x * x, axis=-1, keepdims=True)
    return x * lax.rsqrt(ms + EPS) * g


def _head_rms(x, g, seg):
    ss = _split_dot(x * x, seg)
    return x * lax.rsqrt(ss * (1.0 / HEAD_DIM) + EPS) * g


def _log2_sigmoids(y):
    log_b = jnp.minimum(y, 0.0) - jnp.log2(1.0 + jnp.exp2(-jnp.abs(y)))
    return log_b, log_b - y


def _suffix_ones(n):
    later = lax.broadcasted_iota(jnp.int32, (n, n), 0) > lax.broadcasted_iota(jnp.int32, (n, n), 1)
    return jnp.where(later, 1.0, 0.0).astype(BF16), later


def _ffn(x, shift, scale, gate, g_in, g_out, wgu_ref, wdn_ref, act_ref):
    d_ff = wdn_ref.shape[0]
    h = (_rms(x, g_in) * (1.0 + scale) + shift).astype(BF16)
    for c0 in range(0, d_ff, 2 * MXU_DIM):
        c1 = min(c0 + 2 * MXU_DIM, d_ff)
        g = _dot(h, wgu_ref[:, c0:c1])
        u = _dot(h, wgu_ref[:, d_ff + c0:d_ff + c1])
        act_ref[:, c0:c1] = (g * _sigmoid(g) * u).astype(BF16)
    r = _dot(act_ref[...], wdn_ref[...])
    return x + FFN_RES * gate * _rms(r, g_out)


def _mod_kernel(c_ref, w_ref, b_ref, o_ref):
    c = c_ref[...]
    o_ref[...] = _dot((c * _sigmoid(c)).astype(BF16), w_ref[...]) + b_ref[...]


def _mod_call(c_all, w_c, b_c):
    n, d = c_all.shape
    width = w_c.shape[1]
    return pl.pallas_call(
        _mod_kernel,
        grid=(width // d,),
        in_specs=[pl.BlockSpec((n, d), lambda j: (0, 0)),
                  pl.BlockSpec((d, d), lambda j: (0, j)),
                  pl.BlockSpec((1, d), lambda j: (0, j))],
        out_specs=pl.BlockSpec((n, d), lambda j: (0, j)),
        out_shape=jax.ShapeDtypeStruct((n, width), F32),
        name="mod",
    )(c_all, w_c, b_c.reshape(1, width))


def _ffn_mix_in_kernel(sample, x_ref, sh0_ref, sc0_ref, ga0_ref, sh1_ref, sc1_ref,
                       g0_ref, g1_ref, g2_ref, wgu_ref, wdn_ref, win_ref,
                       vg_ref, og_ref, seg_ref, ws_ref, bs_ref, *rest):
    if sample:
        x1_ref, k_ref, v_ref, qb_ref, gv_ref, mgn_ref, act_ref = rest
    else:
        x1_ref, kt_ref, vt_ref, qb_ref, ktb_ref, vb_ref, mgn_ref, act_ref, mg_ref = rest
    tm = x_ref.shape[0]
    sbw = seg_ref.shape[0]

    x1 = _ffn(x_ref[...], sh0_ref[...], sc0_ref[...], ga0_ref[...], g0_ref[...], g1_ref[...],
              wgu_ref, wdn_ref, act_ref)
    x1_ref[...] = x1

    h = (_rms(x1, g2_ref[...]) * (1.0 + sc1_ref[...]) + sh1_ref[...]).astype(BF16)
    qb_ref[...] = (_dot(h, win_ref[:, 0:sbw]) * (HEAD_DIM ** -0.5 * LOG2E)).astype(BF16)
    k = _dot(h, win_ref[:, sbw:2 * sbw])
    v = _dot(h, win_ref[:, 2 * sbw:3 * sbw])
    u = _dot(h, win_ref[:, 3 * sbw:4 * sbw])
    gv = _head_rms(_dot(h, win_ref[:, 4 * sbw:5 * sbw]), vg_ref[...], seg_ref[...])

    if sample:
        k_ref[...] = k
        v_ref[...] = v
        gv_ref[...] = gv
        mg = u * (gv * ws_ref[...] + bs_ref[...])
    else:
        kt = k.T
        kt_ref[...] = kt
        ktb_ref[...] = kt.astype(BF16)
        vt_ref[...] = v.T
        vb_ref[...] = v.astype(BF16)
        even_head = (lax.broadcasted_iota(jnp.int32, (1, sbw), 1) // HEAD_DIM) % 2 == 0
        gv_even = jnp.where(even_head, gv, 0.0).astype(BF16)
        gv_odd = jnp.where(even_head, 0.0, gv).astype(BF16)
        row = lax.broadcasted_iota(jnp.int32, (CHUNK, CHUNK), 0)
        col = lax.broadcasted_iota(jnp.int32, (CHUNK, CHUNK), 1)
        tril = row >= col
        for p in range(sbw // LANES):
            wp = jnp.concatenate([jnp.where(tril, ws_ref[2 * p], 0.0),
                                  jnp.where(tril, ws_ref[2 * p + 1], 0.0)], axis=1).astype(BF16)
            cols = slice(p * LANES, (p + 1) * LANES)
            for r0 in range(0, tm, CHUNK):
                rows = slice(r0, r0 + CHUNK)
                stacked = jnp.concatenate([gv_even[rows, cols], gv_odd[rows, cols]], axis=0)
                mixed = _dot(wp, stacked) + bs_ref[:, cols]
                mg_ref[rows, cols] = u[rows, cols] * mixed
        mg = mg_ref[...]
    mgn_ref[...] = _head_rms(mg, og_ref[...], seg_ref[...]).astype(BF16)


def _resident(shape):
    nd = len(shape)
    return pl.BlockSpec(shape, lambda i: (0,) * nd, pipeline_mode=pl.Buffered(1))


def _mod_specs(mod, cols, tm, rows_per_mod):
    d = mod.shape[-1] // (3 * N_SUB)
    if mod.ndim == 3:
        tiles = rows_per_mod // tm
        return [pl.BlockSpec((None, 1, d), functools.partial(lambda i, c: (i // tiles, 0, c), c=c))
                for c in cols]
    return [pl.BlockSpec((tm, d), functools.partial(lambda i, c: (i, c), c=c)) for c in cols]


def _norm_specs(idx, d):
    return [pl.BlockSpec((None, 1, d), functools.partial(lambda i, c: (c, 0, 0), c=c),
                         pipeline_mode=pl.Buffered(1)) for c in idx]


def _ffn_mix_in_call(sample, x, mod, rows_per_mod, norm_g3, wgu, wdn, win, vg, og, seg, ws, bs, tm):
    n, d = x.shape
    d_ff = wdn.shape[0]
    sbw = seg.shape[0]
    row = lambda w: pl.BlockSpec((tm, w), lambda i: (i, 0))
    in_specs = ([row(d)] + _mod_specs(mod, (0, 1, 2, 3, 4), tm, rows_per_mod)
                + _norm_specs((0, 1, 2), d)
                + [_resident(a.shape) for a in (wgu, wdn, win, vg, og, seg, ws, bs)])
    f32_out = jax.ShapeDtypeStruct((n, sbw), F32)
    bf_out = jax.ShapeDtypeStruct((n, sbw), BF16)
    if sample:
        out_shape = [jax.ShapeDtypeStruct((n, d), F32), f32_out, f32_out, bf_out, f32_out, bf_out]
        out_specs = [row(s.shape[1]) for s in out_shape]
        scratch = [pltpu.VMEM((tm, d_ff), BF16)]
    else:
        tiles = rows_per_mod // tm
        n_seq = n // rows_per_mod
        t_out = lambda dt: jax.ShapeDtypeStruct((n_seq, sbw, rows_per_mod), dt)
        t_spec = pl.BlockSpec((None, sbw, tm), lambda i: (i // tiles, 0, i % tiles))
        out_shape = [jax.ShapeDtypeStruct((n, d), F32), t_out(F32), t_out(F32), bf_out, t_out(BF16),
                     bf_out, bf_out]
        out_specs = [row(d), t_spec, t_spec, row(sbw), t_spec, row(sbw), row(sbw)]
        scratch = [pltpu.VMEM((tm, d_ff), BF16), pltpu.VMEM((tm, sbw), F32)]
    return pl.pallas_call(
        functools.partial(_ffn_mix_in_kernel, sample),
        grid=(n // tm,),
        in_specs=in_specs,
        out_specs=out_specs,
        out_shape=out_shape,
        scratch_shapes=scratch,
        compiler_params=pltpu.CompilerParams(dimension_semantics=("arbitrary",),
                                             vmem_limit_bytes=VMEM_LIMIT_BYTES),
        name="ffn_mix_in_sample" if sample else "ffn_mix_in",
    )(x, mod, mod, mod, mod, mod, norm_g3, norm_g3, norm_g3, wgu, wdn, win, vg, og, seg, ws, bs)


def _prompt_tile(qi, pair, side_jobs, bias_ref, q_ref, kt_ref, v_ref, g_ref, o_ref,
                 acc_ref, r_ref, lb_ref, l1_ref, tot_ref):
    t, h = ATTN_TILE, KEY_BLOCK
    side_jobs = list(side_jobs)

    def piece_done():
        if side_jobs:
            side_jobs.pop(0)()

    lane = lax.broadcasted_iota(jnp.int32, (1, LANES), 1)
    keep = (jnp.where(lane < HEAD_DIM, 1.0, 0.0).astype(BF16),
            jnp.where(lane < HEAD_DIM, 0.0, 1.0).astype(BF16))
    tmat, causal = _suffix_ones(h)
    si = lax.broadcasted_iota(jnp.int32, (LANES, LANES), 0) // HEAD_DIM
    sj = lax.broadcasted_iota(jnp.int32, (LANES, LANES), 1) // HEAD_DIM
    seg = jnp.where(si == sj, 1.0, 0.0).astype(BF16)
    biases = (bias_ref[2 * pair] * LOG2E, bias_ref[2 * pair + 1] * LOG2E)
    halves = (slice(0, h), slice(h, t))

    def n_blocks(diag, half):
        return 1 if diag and half == 0 else t // h

    def scores(slot, qm, kt, bias, diag, hh, half):
        nb = n_blocks(diag, half)
        y = _dot(qm[halves[half]], kt[:, :nb * h]) + bias
        log_b, log_1mb = _log2_sigmoids(y)
        for blk in range(nb):
            cols = slice(blk * h, (blk + 1) * h)
            l1 = log_1mb[:, cols]
            if diag and blk == nb - 1:
                l1 = jnp.where(causal, l1, 0.0)
            lb_ref[slot, hh, half, :, cols] = log_b[:, cols]
            l1_ref[slot, hh, half, :, cols] = l1.astype(BF16)
            tot_ref[slot, hh, half, blk] = jnp.sum(l1, axis=-1, keepdims=True)

    def values(slot, vm, diag, hh, half):
        nb = n_blocks(diag, half)
        r = 0.0 if diag else r_ref[hh, halves[half], :]
        weights = [None] * nb
        for blk in range(nb - 1, -1, -1):
            cols = slice(blk * h, (blk + 1) * h)
            between = _dot(l1_ref[slot, hh, half, :, cols], tmat) + r
            a = jnp.exp2(lb_ref[slot, hh, half, :, cols] + between)
            if diag and blk == nb - 1:
                a = jnp.where(causal, a, 0.0)
            weights[blk] = a.astype(BF16)
            r = r + tot_ref[slot, hh, half, blk]
        w = weights[0] if nb == 1 else jnp.concatenate(weights, axis=1)
        pv = _dot(w, vm[:nb * h])
        if diag and hh == 0:
            acc_ref[halves[half], :] = pv
        else:
            acc_ref[halves[half], :] += pv
        r_ref[hh, halves[half], :] = r

    chains = [(hh, half) for hh in range(2) for half in range(2)]
    r0 = qi * t
    q = q_ref[r0:r0 + t, :]
    qms = (q * keep[0], q * keep[1])
    kt = kt_ref[:, r0:r0 + t]
    for hh, half in chains:
        scores(0, qms[hh], kt, biases[hh], True, hh, half)
        piece_done()
    for k in range(qi + 1):
        k0 = r0 - k * t
        v = v_ref[k0:k0 + t, :]
        vms = (v * keep[0], v * keep[1])
        if k < qi:
            kt = kt_ref[:, k0 - t:k0]
        for hh, half in chains:
            values(k % 2, vms[hh], k == 0, hh, half)
            piece_done()
            if k < qi:
                scores((k + 1) % 2, qms[hh], kt, biases[hh], False, hh, half)
    while side_jobs:
        piece_done()
    o_ref[r0:r0 + t, :] = _head_rms(acc_ref[...], g_ref[...], seg).astype(BF16)


def _sample_jobs(first, last, bias_ref, q_ref, g_ref, seg_ref, k_refs, v_refs, o_ref,
                 acc_ref, r_ref):
    n_heads, sbw = acc_ref.shape
    h = KEY_BLOCK
    n_pages = len(k_refs)
    n_blocks = n_pages * PAGE_SIZE // h
    state = {}

    def own_mask():
        head_of_row = lax.broadcasted_iota(jnp.int32, (n_heads, sbw), 0)
        head_of_col = lax.broadcasted_iota(jnp.int32, (n_heads, sbw), 1) // HEAD_DIM
        return head_of_row == head_of_col

    def score_job():
        q = jnp.broadcast_to(q_ref[...].astype(F32), (n_heads, sbw))
        qm = jnp.where(own_mask(), q, 0.0).astype(BF16)
        keys = jnp.concatenate([k_refs[i][...].astype(BF16) for i in range(n_pages)], axis=1)
        state["y"] = _dot(qm, keys) + bias_ref[...] * LOG2E

    def weight_job():
        tmat, _ = _suffix_ones(h)
        log_b, log_1mb = _log2_sigmoids(state.pop("y"))
        stacked = jnp.concatenate([log_1mb[:, i * h:(i + 1) * h] for i in range(n_blocks)], axis=0)
        within = _dot(stacked.astype(BF16), tmat)
        totals = jnp.sum(stacked, axis=-1, keepdims=True)
        r = 0.0 if first else r_ref[...]
        between = [None] * n_blocks
        for i in range(n_blocks - 1, -1, -1):
            rows = slice(i * n_heads, (i + 1) * n_heads)
            between[i] = within[rows] + r
            r = r + totals[rows]
        state["a"] = jnp.exp2(log_b + jnp.concatenate(between, axis=1)).astype(BF16)
        if not last:
            r_ref[...] = r

    def value_job():
        vals = jnp.concatenate([v_refs[i][...].astype(BF16) for i in range(n_pages)], axis=1)
        acc = _dot_nt(state.pop("a"), vals)
        if not first:
            acc = acc + acc_ref[...]
        if last:
            o = jnp.sum(jnp.where(own_mask(), acc, 0.0), axis=0, keepdims=True)
            o_ref[...] = _head_rms(o, g_ref[...], seg_ref[...]).astype(BF16)
        else:
            acc_ref[...] = acc

    return [score_job, weight_job, value_job]


def _sb_kernel(n_pages, pt_ref, bias_ref, bias_col_ref, q_ref, kt_ref, v_ref, g_ref,
               qs_ref, gs_ref, seg_ref, *rest):
    k_refs, v_refs = rest[:n_pages], rest[n_pages:2 * n_pages]
    o_ref, os_ref, acc_ref, r_ref, lb_ref, l1_ref, tot_ref, accs_ref, rs_ref = rest[2 * n_pages:]
    pair = pl.program_id(1)
    step = pl.program_id(2)
    n_steps = q_ref.shape[0] // ATTN_TILE
    for j in range(n_steps):
        @pl.when(step == j)
        def _():
            jobs = _sample_jobs(j == 0, j == n_steps - 1, bias_col_ref, qs_ref, gs_ref, seg_ref,
                                k_refs, v_refs, os_ref, accs_ref, rs_ref)
            _prompt_tile(j, pair, jobs, bias_ref, q_ref, kt_ref, v_ref, g_ref, o_ref,
                         acc_ref, r_ref, lb_ref, l1_ref, tot_ref)


def _sb_call(qb, ktb, vb, bias, g, seq, page_table, qsb, seg, pool_k, pool_v):
    n, sbw = qb.shape
    nb_s, n_pages = page_table.shape
    n_heads = sbw // HEAD_DIM
    n_pairs = sbw // LANES
    n_steps = seq // ATTN_TILE
    assert nb_s == (n // seq) * n_pairs and n_pages % n_steps == 0
    pg = n_pages // n_steps
    pool_k = pool_k.transpose(0, 2, 3, 1).reshape(pool_k.shape[0], sbw, PAGE_SIZE)
    pool_v = pool_v.transpose(0, 2, 3, 1).reshape(pool_v.shape[0], sbw, PAGE_SIZE)

    def page_spec(i):
        def index(b, p, s, pt):
            return (pt[(b * n_pairs + p) * n_pages + (n_steps - 1 - s) * pg + i], 0, 0)
        return pl.BlockSpec((None, sbw, PAGE_SIZE), index)

    blk = lambda: pl.BlockSpec((seq, LANES), lambda b, p, s, pt: (b, p))
    whole = lambda shape: pl.BlockSpec(shape, lambda b, p, s, pt: (0,) * len(shape))
    per_sample = lambda: pl.BlockSpec((None, 1, sbw), lambda b, p, s, pt: (b * n_pairs + p, 0, 0))
    grid_spec = pltpu.PrefetchScalarGridSpec(
        num_scalar_prefetch=1,
        grid=(n // seq, n_pairs, n_steps),
        in_specs=([pl.BlockSpec(memory_space=pltpu.SMEM), whole((n_heads, 1)), blk(),
                   pl.BlockSpec((None, LANES, seq), lambda b, p, s, pt: (b, p, 0)), blk(),
                   pl.BlockSpec((1, LANES), lambda b, p, s, pt: (0, p)),
                   per_sample(), whole((1, sbw)), whole((sbw, sbw))]
                  + [page_spec(i) for i in range(pg)] * 2),
        out_specs=[blk(), per_sample()],
        scratch_shapes=[pltpu.VMEM((ATTN_TILE, LANES), F32), pltpu.VMEM((2, ATTN_TILE, 1), F32),
                        pltpu.VMEM((2, 2, 2, KEY_BLOCK, ATTN_TILE), F32),
                        pltpu.VMEM((2, 2, 2, KEY_BLOCK, ATTN_TILE), BF16),
                        pltpu.VMEM((2, 2, 2, ATTN_TILE // KEY_BLOCK, KEY_BLOCK, 1), F32),
                        pltpu.VMEM((n_heads, sbw), F32), pltpu.VMEM((n_heads, 1), F32)],
    )
    sbn, sbn_s = pl.pallas_call(
        functools.partial(_sb_kernel, pg),
        grid_spec=grid_spec,
        out_shape=[jax.ShapeDtypeStruct((n, sbw), BF16),
                   jax.ShapeDtypeStruct((nb_s, 1, sbw), BF16)],
        compiler_params=pltpu.CompilerParams(
            dimension_semantics=("arbitrary", "arbitrary", "arbitrary"),
            vmem_limit_bytes=VMEM_LIMIT_BYTES),
        name="sb",
    )(page_table.reshape(-1), bias, bias.reshape(n_heads, 1), qb, ktb, vb, g,
      qsb.reshape(nb_s, 1, sbw), g, seg, *([pool_k] * pg), *([pool_v] * pg))
    return sbn, sbn_s.reshape(nb_s, sbw)


def _mix_out_ffn_kernel(x_ref, sbn_ref, mgn_ref, ga1_ref, sh2_ref, sc2_ref, ga2_ref,
                        g3_ref, g4_ref, g5_ref, wout_ref, wgu_ref, wdn_ref, y_ref, act_ref):
    sbw = sbn_ref.shape[1]
    o = _dot(sbn_ref[...], wout_ref[0:sbw, :]) + _dot(mgn_ref[...], wout_ref[sbw:, :])
    x2 = x_ref[...] + ga1_ref[...] * _rms(o, g3_ref[...])
    y_ref[...] = _ffn(x2, sh2_ref[...], sc2_ref[...], ga2_ref[...], g4_ref[...], g5_ref[...],
                      wgu_ref, wdn_ref, act_ref)


def _mix_out_ffn_call(sample, x, sbn, mgn, mod, rows_per_mod, norm_g3, wout, wgu, wdn, tm):
    n, d = x.shape
    d_ff = wdn.shape[0]
    row = lambda w: pl.BlockSpec((tm, w), lambda i: (i, 0))
    in_specs = ([row(d), row(sbn.shape[1]), row(mgn.shape[1])]
                + _mod_specs(mod, (5, 6, 7, 8), tm, rows_per_mod)
                + _norm_specs((3, 4, 5), d)
                + [_resident(a.shape) for a in (wout, wgu, wdn)])
    return pl.pallas_call(
        _mix_out_ffn_kernel,
        grid=(n // tm,),
        in_specs=in_specs,
        out_specs=row(d),
        out_shape=jax.ShapeDtypeStruct((n, d), F32),
        scratch_shapes=[pltpu.VMEM((tm, d_ff), BF16)],
        compiler_params=pltpu.CompilerParams(dimension_semantics=("arbitrary",),
                                             vmem_limit_bytes=VMEM_LIMIT_BYTES),
        name="mix_out_ffn_sample" if sample else "mix_out_ffn",
    )(x, sbn, mgn, mod, mod, mod, mod, norm_g3, norm_g3, norm_g3, wout, wgu, wdn)


def kernel(x_prompt, x_sample, c_prompt, c_sample, cache_k, cache_v, page_table, w_c, b_c, norm_g, w_in, sb_bias, mlp_v_g, w_s, b_s, sb_out_g, mlp_out_g, w_out, w1_gu, w1_dn, w2_gu, w2_dn):
    depth = w_c.shape[0]
    nb_p, seq, d = x_prompt.shape
    nb_s, dec_seq, _ = x_sample.shape
    n_sb, hd = sb_out_g.shape[1:]
    n_mlp = mlp_v_g.shape[1]
    sbw = n_sb * hd
    assert hd == HEAD_DIM and n_mlp * hd == sbw and dec_seq == 1
    assert w_s.shape[2] == CHUNK and cache_k.shape[2] == PAGE_SIZE
    assert seq % ATTN_TILE == 0 and seq % ROW_TILE == 0 and ROW_TILE % CHUNK == 0

    head_id = jnp.arange(sbw) // HEAD_DIM
    seg = (head_id[:, None] == head_id[None, :]).astype(BF16)
    flat = lambda a: a.reshape(1, sbw)

    xp = x_prompt.reshape(nb_p * seq, d)
    xs = x_sample.reshape(nb_s, d)
    c_all = jnp.concatenate([c_prompt, c_sample], axis=0)
    outs = [[] for _ in range(5)]
    for l in range(depth):
        mod = _mod_call(c_all, w_c[l].astype(BF16), b_c[l])
        mod_p = mod[:nb_p].reshape(nb_p, 1, -1)
        mod_s = mod[nb_p:]
        norm_g3 = norm_g[l].reshape(2 * N_SUB, 1, d)
        w1 = (w1_gu[l].astype(BF16), w1_dn[l].astype(BF16))
        w2 = (w2_gu[l].astype(BF16), w2_dn[l].astype(BF16))
        win = w_in[l].astype(BF16)
        wout = w_out[l].astype(BF16)
        vg, og, sg = flat(mlp_v_g[l]), flat(mlp_out_g[l]), flat(sb_out_g[l])
        bs_rows = jnp.repeat(b_s[l].T, HEAD_DIM, axis=1)
        ws0 = flat(jnp.repeat(w_s[l][:, 0, 0], HEAD_DIM))
        bs0 = bs_rows[0:1]

        x1p, ktp, vtp, qbp, ktbp, vbp, mgnp = _ffn_mix_in_call(
            False, xp, mod_p, seq, norm_g3, *w1, win, vg, og, seg, w_s[l], bs_rows, ROW_TILE)
        x1s, ks, vs, qbs, gvs, mgns = _ffn_mix_in_call(
            True, xs, mod_s, 1, norm_g3, *w1, win, vg, og, seg, ws0, bs0, nb_s)

        sbnp, sbns = _sb_call(qbp, ktbp, vbp, sb_bias[l], sg, seq, page_table, qbs, seg,
                              cache_k[l], cache_v[l])

        xp = _mix_out_ffn_call(False, x1p, sbnp, mgnp, mod_p, seq, norm_g3, wout, *w2, ROW_TILE)
        xs = _mix_out_ffn_call(True, x1s, sbns, mgns, mod_s, 1, norm_g3, wout, *w2, nb_s)

        outs[0].append(ktp.reshape(nb_p, n_sb, hd, seq).transpose(0, 3, 1, 2))
        outs[1].append(vtp.reshape(nb_p, n_sb, hd, seq).transpose(0, 3, 1, 2))
        outs[2].append(ks.reshape(nb_s, 1, n_sb, hd))
        outs[3].append(vs.reshape(nb_s, 1, n_sb, hd))
        outs[4].append(gvs.reshape(nb_s, 1, n_mlp, hd))
    return (xp.reshape(nb_p, seq, d), xs.reshape(nb_s, 1, d)) + tuple(jnp.stack(o) for o in outs)
```

```python
import functools

import jax
import jax.numpy as jnp
from jax import lax
from jax.experimental import pallas as pl
from jax.experimental.pallas import tpu as pltpu

F32 = jnp.float32
BF16 = jnp.bfloat16

HEAD_DIM = 64
CHUNK = 128
PAGE_SIZE = 128
N_SUB = 3
FFN_RES = 0.5
EPS = 1e-6

LANES = 128
MXU_DIM = 256
VMEM_LIMIT_BYTES = 56 * 1024 * 1024

ROW_TILE = 512
KEY_BLOCK = MXU_DIM
ATTN_TILE = 2 * KEY_BLOCK
LOG2E = 1.4426950408889634


def _dot(a, b):
    return jnp.dot(a, b, preferred_element_type=F32)


def _dot_nt(a, b):
    return lax.dot_general(a, b, (((1,), (1,)), ((), ())), preferred_element_type=F32)


def _split_dot(x, m):
    hi = x.astype(BF16)
    lo = (x - hi.astype(F32)).astype(BF16)
    return _dot(hi, m) + _dot(lo, m)


def _sigmoid(x):
    return 1.0 / (1.0 + jnp.exp(-x))


def _rms(x, g):
    ms = jnp.mean(x * x, axis=-1, keepdims=True)
    return x * lax.rsqrt(ms + EPS) * g


def _head_rms(x, g, seg):
    ss = _split_dot(x * x, seg)
    return x * lax.rsqrt(ss * (1.0 / HEAD_DIM) + EPS) * g


def _log2_sigmoids(y):
    log_b = jnp.minimum(y, 0.0) - jnp.log2(1.0 + jnp.exp2(-jnp.abs(y)))
    return log_b, log_b - y


def _suffix_ones(n):
    later = lax.broadcasted_iota(jnp.int32, (n, n), 0) > lax.broadcasted_iota(jnp.int32, (n, n), 1)
    return jnp.where(later, 1.0, 0.0).astype(BF16), later


def _ffn(x, shift, scale, gate, g_in, g_out, wgu_ref, wdn_ref, act_ref):
    d_ff = wdn_ref.shape[0]
    h = (_rms(x, g_in) * (1.0 + scale) + shift).astype(BF16)
    for c0 in range(0, d_ff, 2 * MXU_DIM):
        c1 = min(c0 + 2 * MXU_DIM, d_ff)
        g = _dot(h, wgu_ref[:, c0:c1])
        u = _dot(h, wgu_ref[:, d_ff + c0:d_ff + c1])
        act_ref[:, c0:c1] = (g * _sigmoid(g) * u).astype(BF16)
    r = _dot(act_ref[...], wdn_ref[...])
    return x + FFN_RES * gate * _rms(r, g_out)


def _mod_kernel(c_ref, w_ref, b_ref, o_ref):
    c = c_ref[...]
    o_ref[...] = _dot((c * _sigmoid(c)).astype(BF16), w_ref[...]) + b_ref[...]


def _mod_call(c_all, w_c, b_c):
    n, d = c_all.shape
    width = w_c.shape[1]
    return pl.pallas_call(
        _mod_kernel,
        grid=(width // d,),
        in_specs=[pl.BlockSpec((n, d), lambda j: (0, 0)),
                  pl.BlockSpec((d, d), lambda j: (0, j)),
                  pl.BlockSpec((1, d), lambda j: (0, j))],
        out_specs=pl.BlockSpec((n, d), lambda j: (0, j)),
        out_shape=jax.ShapeDtypeStruct((n, width), F32),
        name="mod",
    )(c_all, w_c, b_c.reshape(1, width))


def _ffn_mix_in_kernel(sample, x_ref, sh0_ref, sc0_ref, ga0_ref, sh1_ref, sc1_ref,
                       g0_ref, g1_ref, g2_ref, wgu_ref, wdn_ref, win_ref,
                       vg_ref, og_ref, seg_ref, ws_ref, bs_ref, *rest):
    if sample:
        x1_ref, k_ref, v_ref, qb_ref, gv_ref, mgn_ref, act_ref = rest
    else:
        x1_ref, kt_ref, vt_ref, qb_ref, ktb_ref, vb_ref, mgn_ref, act_ref, mg_ref = rest
    tm = x_ref.shape[0]
    sbw = seg_ref.shape[0]

    x1 = _ffn(x_ref[...], sh0_ref[...], sc0_ref[...], ga0_ref[...], g0_ref[...], g1_ref[...],
              wgu_ref, wdn_ref, act_ref)
    x1_ref[...] = x1

    h = (_rms(x1, g2_ref[...]) * (1.0 + sc1_ref[...]) + sh1_ref[...]).astype(BF16)
    qb_ref[...] = (_dot(h, win_ref[:, 0:sbw]) * (HEAD_DIM ** -0.5 * LOG2E)).astype(BF16)
    k = _dot(h, win_ref[:, sbw:2 * sbw])
    v = _dot(h, win_ref[:, 2 * sbw:3 * sbw])
    u = _dot(h, win_ref[:, 3 * sbw:4 * sbw])
    gv = _head_rms(_dot(h, win_ref[:, 4 * sbw:5 * sbw]), vg_ref[...], seg_ref[...])

    if sample:
        k_ref[...] = k
        v_ref[...] = v
        gv_ref[...] = gv
        mg = u * (gv * ws_ref[...] + bs_ref[...])
    else:
        kt = k.T
        kt_ref[...] = kt
        ktb_ref[...] = kt.astype(BF16)
        vt_ref[...] = v.T
        vb_ref[...] = v.astype(BF16)
        even_head = (lax.broadcasted_iota(jnp.int32, (1, sbw), 1) // HEAD_DIM) % 2 == 0
        gv_even = jnp.where(even_head, gv, 0.0).astype(BF16)
        gv_odd = jnp.where(even_head, 0.0, gv).astype(BF16)
        row = lax.broadcasted_iota(jnp.int32, (CHUNK, CHUNK), 0)
        col = lax.broadcasted_iota(jnp.int32, (CHUNK, CHUNK), 1)
        tril = row >= col
        for p in range(sbw // LANES):
            wp = jnp.concatenate([jnp.where(tril, ws_ref[2 * p], 0.0),
                                  jnp.where(tril, ws_ref[2 * p + 1], 0.0)], axis=1).astype(BF16)
            cols = slice(p * LANES, (p + 1) * LANES)
            for r0 in range(0, tm, CHUNK):
                rows = slice(r0, r0 + CHUNK)
                stacked = jnp.concatenate([gv_even[rows, cols], gv_odd[rows, cols]], axis=0)
                mixed = _dot(wp, stacked) + bs_ref[:, cols]
                mg_ref[rows, cols] = u[rows, cols] * mixed
        mg = mg_ref[...]
    mgn_ref[...] = _head_rms(mg, og_ref[...], seg_ref[...]).astype(BF16)


def _resident(shape):
    nd = len(shape)
    return pl.BlockSpec(shape, lambda i: (0,) * nd, pipeline_mode=pl.Buffered(1))


def _mod_specs(mod, cols, tm, rows_per_mod):
    d = mod.shape[-1] // (3 * N_SUB)
    if mod.ndim == 3:
        tiles = rows_per_mod // tm
        return [pl.BlockSpec((None, 1, d), functools.partial(lambda i, c: (i // tiles, 0, c), c=c))
                for c in cols]
    return [pl.BlockSpec((tm, d), functools.partial(lambda i, c: (i, c), c=c)) for c in cols]


def _norm_specs(idx, d):
    return [pl.BlockSpec((None, 1, d), functools.partial(lambda i, c: (c, 0, 0), c=c),
                         pipeline_mode=pl.Buffered(1)) for c in idx]


def _ffn_mix_in_call(sample, x, mod, rows_per_mod, norm_g3, wgu, wdn, win, vg, og, seg, ws, bs, tm):
    n, d = x.shape
    d_ff = wdn.shape[0]
    sbw = seg.shape[0]
    row = lambda w: pl.BlockSpec((tm, w), lambda i: (i, 0))
    in_specs = ([row(d)] + _mod_specs(mod, (0, 1, 2, 3, 4), tm, rows_per_mod)
                + _norm_specs((0, 1, 2), d)
                + [_resident(a.shape) for a in (wgu, wdn, win, vg, og, seg, ws, bs)])
    f32_out = jax.ShapeDtypeStruct((n, sbw), F32)
    bf_out = jax.ShapeDtypeStruct((n, sbw), BF16)
    if sample:
        out_shape = [jax.ShapeDtypeStruct((n, d), F32), f32_out, f32_out, bf_out, f32_out, bf_out]
        out_specs = [row(s.shape[1]) for s in out_shape]
        scratch = [pltpu.VMEM((tm, d_ff), BF16)]
    else:
        tiles = rows_per_mod // tm
        n_seq = n // rows_per_mod
        t_out = lambda dt: jax.ShapeDtypeStruct((n_seq, sbw, rows_per_mod), dt)
        t_spec = pl.BlockSpec((None, sbw, tm), lambda i: (i // tiles, 0, i % tiles))
        out_shape = [jax.ShapeDtypeStruct((n, d), F32), t_out(F32), t_out(F32), bf_out, t_out(BF16),
                     bf_out, bf_out]
        out_specs = [row(d), t_spec, t_spec, row(sbw), t_spec, row(sbw), row(sbw)]
        scratch = [pltpu.VMEM((tm, d_ff), BF16), pltpu.VMEM((tm, sbw), F32)]
    return pl.pallas_call(
        functools.partial(_ffn_mix_in_kernel, sample),
        grid=(n // tm,),
        in_specs=in_specs,
        out_specs=out_specs,
        out_shape=out_shape,
        scratch_shapes=scratch,
        compiler_params=pltpu.CompilerParams(dimension_semantics=("arbitrary",),
                                             vmem_limit_bytes=VMEM_LIMIT_BYTES),
        name="ffn_mix_in_sample" if sample else "ffn_mix_in",
    )(x, mod, mod, mod, mod, mod, norm_g3, norm_g3, norm_g3, wgu, wdn, win, vg, og, seg, ws, bs)


def _prompt_tile(qi, pair, side_jobs, bias_ref, q_ref, kt_ref, v_ref, g_ref, o_ref,
                 acc_ref, r_ref, lb_ref, l1_ref, tot_ref):
    t, h = ATTN_TILE, KEY_BLOCK
    side_jobs = list(side_jobs)

    pieces = [0]

    def piece_done():
        pieces[0] += 1
        if side_jobs and pieces[0] % 2 == 1:
            side_jobs.pop(0)()

    lane = lax.broadcasted_iota(jnp.int32, (1, LANES), 1)
    keep = (jnp.where(lane < HEAD_DIM, 1.0, 0.0).astype(BF16),
            jnp.where(lane < HEAD_DIM, 0.0, 1.0).astype(BF16))
    tmat, causal = _suffix_ones(h)
    si = lax.broadcasted_iota(jnp.int32, (LANES, LANES), 0) // HEAD_DIM
    sj = lax.broadcasted_iota(jnp.int32, (LANES, LANES), 1) // HEAD_DIM
    seg = jnp.where(si == sj, 1.0, 0.0).astype(BF16)
    biases = (bias_ref[2 * pair] * LOG2E, bias_ref[2 * pair + 1] * LOG2E)
    halves = (slice(0, h), slice(h, t))

    def n_blocks(diag, half):
        return 1 if diag and half == 0 else t // h

    def scores(slot, qm, kt, bias, diag, hh, half):
        nb = n_blocks(diag, half)
        y = _dot(qm[halves[half]], kt[:, :nb * h]) + bias
        log_b, log_1mb = _log2_sigmoids(y)
        for blk in range(nb):
            cols = slice(blk * h, (blk + 1) * h)
            l1 = log_1mb[:, cols]
            if diag and blk == nb - 1:
                l1 = jnp.where(causal, l1, 0.0)
            lb_ref[slot, hh, half, :, cols] = log_b[:, cols]
            l1_ref[slot, hh, half, :, cols] = l1.astype(BF16)
            tot_ref[slot, hh, half, blk] = jnp.sum(l1, axis=-1, keepdims=True)

    def values(slot, vm, diag, hh, half):
        nb = n_blocks(diag, half)
        r = 0.0 if diag else r_ref[hh, halves[half], :]
        weights = [None] * nb
        for blk in range(nb - 1, -1, -1):
            cols = slice(blk * h, (blk + 1) * h)
            between = _dot(l1_ref[slot, hh, half, :, cols], tmat) + r
            a = jnp.exp2(lb_ref[slot, hh, half, :, cols] + between)
            if diag and blk == nb - 1:
                a = jnp.where(causal, a, 0.0)
            weights[blk] = a.astype(BF16)
            r = r + tot_ref[slot, hh, half, blk]
        w = weights[0] if nb == 1 else jnp.concatenate(weights, axis=1)
        pv = _dot(w, vm[:nb * h])
        if diag and hh == 0:
            acc_ref[halves[half], :] = pv
        else:
            acc_ref[halves[half], :] += pv
        r_ref[hh, halves[half], :] = r

    chains = [(hh, half) for hh in range(2) for half in range(2)]
    r0 = qi * t
    q = q_ref[r0:r0 + t, :]
    qms = (q * keep[0], q * keep[1])
    kt = kt_ref[:, r0:r0 + t]
    for hh, half in chains:
        scores(0, qms[hh], kt, biases[hh], True, hh, half)
        piece_done()
    for k in range(qi + 1):
        k0 = r0 - k * t
        v = v_ref[k0:k0 + t, :]
        vms = (v * keep[0], v * keep[1])
        if k < qi:
            kt = kt_ref[:, k0 - t:k0]
        for hh, half in chains:
            values(k % 2, vms[hh], k == 0, hh, half)
            piece_done()
            if k < qi:
                scores((k + 1) % 2, qms[hh], kt, biases[hh], False, hh, half)
    while side_jobs:
        side_jobs.pop(0)()
    o_ref[r0:r0 + t, :] = _head_rms(acc_ref[...], g_ref[...], seg).astype(BF16)


def _sample_jobs(first, last, bias_ref, q_ref, g_ref, seg_ref, k_refs, v_refs, o_ref,
                 acc_ref, r_ref):
    n_heads, sbw = acc_ref.shape
    h = KEY_BLOCK
    n_pages = len(k_refs)
    n_blocks = n_pages * PAGE_SIZE // h
    state = {}

    def own_mask():
        head_of_row = lax.broadcasted_iota(jnp.int32, (n_heads, sbw), 0)
        head_of_col = lax.broadcasted_iota(jnp.int32, (n_heads, sbw), 1) // HEAD_DIM
        return head_of_row == head_of_col

    def score_job():
        q = jnp.broadcast_to(q_ref[...].astype(F32), (n_heads, sbw))
        qm = jnp.where(own_mask(), q, 0.0).astype(BF16)
        keys = jnp.concatenate([k_refs[i][...].astype(BF16) for i in range(n_pages)], axis=1)
        state["y"] = _dot(qm, keys) + bias_ref[...] * LOG2E

    def weight_job():
        tmat, _ = _suffix_ones(h)
        log_b, log_1mb = _log2_sigmoids(state.pop("y"))
        stacked = jnp.concatenate([log_1mb[:, i * h:(i + 1) * h] for i in range(n_blocks)], axis=0)
        within = _dot(stacked.astype(BF16), tmat)
        totals = jnp.sum(stacked, axis=-1, keepdims=True)
        r = 0.0 if first else r_ref[...]
        between = [None] * n_blocks
        for i in range(n_blocks - 1, -1, -1):
            rows = slice(i * n_heads, (i + 1) * n_heads)
            between[i] = within[rows] + r
            r = r + totals[rows]
        state["a"] = jnp.exp2(log_b + jnp.concatenate(between, axis=1)).astype(BF16)
        if not last:
            r_ref[...] = r

    def value_job():
        vals = jnp.concatenate([v_refs[i][...].astype(BF16) for i in range(n_pages)], axis=1)
        acc = _dot_nt(state.pop("a"), vals)
        if not first:
            acc = acc + acc_ref[...]
        if last:
            o = jnp.sum(jnp.where(own_mask(), acc, 0.0), axis=0, keepdims=True)
            o_ref[...] = _head_rms(o, g_ref[...], seg_ref[...]).astype(BF16)
        else:
            acc_ref[...] = acc

    return [score_job, weight_job, value_job]


def _sb_kernel(n_pages, pt_ref, bias_ref, bias_col_ref, q_ref, kt_ref, v_ref, g_ref,
               qs_ref, gs_ref, seg_ref, *rest):
    k_refs, v_refs = rest[:n_pages], rest[n_pages:2 * n_pages]
    o_ref, os_ref, acc_ref, r_ref, lb_ref, l1_ref, tot_ref, accs_ref, rs_ref = rest[2 * n_pages:]
    pair = pl.program_id(1)
    step = pl.program_id(2)
    n_steps = q_ref.shape[0] // ATTN_TILE
    for j in range(n_steps):
        @pl.when(step == j)
        def _():
            jobs = _sample_jobs(j == 0, j == n_steps - 1, bias_col_ref, qs_ref, gs_ref, seg_ref,
                                k_refs, v_refs, os_ref, accs_ref, rs_ref)
            _prompt_tile(j, pair, jobs, bias_ref, q_ref, kt_ref, v_ref, g_ref, o_ref,
                         acc_ref, r_ref, lb_ref, l1_ref, tot_ref)


def _sb_call(qb, ktb, vb, bias, g, seq, page_table, qsb, seg, pool_k, pool_v):
    n, sbw = qb.shape
    nb_s, n_pages = page_table.shape
    n_heads = sbw // HEAD_DIM
    n_pairs = sbw // LANES
    n_steps = seq // ATTN_TILE
    assert nb_s == (n // seq) * n_pairs and n_pages % n_steps == 0
    pg = n_pages // n_steps
    pool_k = pool_k.transpose(0, 2, 3, 1).reshape(pool_k.shape[0], sbw, PAGE_SIZE)
    pool_v = pool_v.transpose(0, 2, 3, 1).reshape(pool_v.shape[0], sbw, PAGE_SIZE)

    def page_spec(i):
        def index(b, p, s, pt):
            return (pt[(b * n_pairs + p) * n_pages + (n_steps - 1 - s) * pg + i], 0, 0)
        return pl.BlockSpec((None, sbw, PAGE_SIZE), index)

    blk = lambda: pl.BlockSpec((seq, LANES), lambda b, p, s, pt: (b, p))
    whole = lambda shape: pl.BlockSpec(shape, lambda b, p, s, pt: (0,) * len(shape))
    per_sample = lambda: pl.BlockSpec((None, 1, sbw), lambda b, p, s, pt: (b * n_pairs + p, 0, 0))
    grid_spec = pltpu.PrefetchScalarGridSpec(
        num_scalar_prefetch=1,
        grid=(n // seq, n_pairs, n_steps),
        in_specs=([pl.BlockSpec(memory_space=pltpu.SMEM), whole((n_heads, 1)), blk(),
                   pl.BlockSpec((None, LANES, seq), lambda b, p, s, pt: (b, p, 0)), blk(),
                   pl.BlockSpec((1, LANES), lambda b, p, s, pt: (0, p)),
                   per_sample(), whole((1, sbw)), whole((sbw, sbw))]
                  + [page_spec(i) for i in range(pg)] * 2),
        out_specs=[blk(), per_sample()],
        scratch_shapes=[pltpu.VMEM((ATTN_TILE, LANES), F32), pltpu.VMEM((2, ATTN_TILE, 1), F32),
                        pltpu.VMEM((2, 2, 2, KEY_BLOCK, ATTN_TILE), F32),
                        pltpu.VMEM((2, 2, 2, KEY_BLOCK, ATTN_TILE), BF16),
                        pltpu.VMEM((2, 2, 2, ATTN_TILE // KEY_BLOCK, KEY_BLOCK, 1), F32),
                        pltpu.VMEM((n_heads, sbw), F32), pltpu.VMEM((n_heads, 1), F32)],
    )
    sbn, sbn_s = pl.pallas_call(
        functools.partial(_sb_kernel, pg),
        grid_spec=grid_spec,
        out_shape=[jax.ShapeDtypeStruct((n, sbw), BF16),
                   jax.ShapeDtypeStruct((nb_s, 1, sbw), BF16)],
        compiler_params=pltpu.CompilerParams(
            dimension_semantics=("arbitrary", "arbitrary", "arbitrary"),
            vmem_limit_bytes=VMEM_LIMIT_BYTES),
        name="sb",
    )(page_table.reshape(-1), bias, bias.reshape(n_heads, 1), qb, ktb, vb, g,
      qsb.reshape(nb_s, 1, sbw), g, seg, *([pool_k] * pg), *([pool_v] * pg))
    return sbn, sbn_s.reshape(nb_s, sbw)


def _mix_out_ffn_kernel(x_ref, sbn_ref, mgn_ref, ga1_ref, sh2_ref, sc2_ref, ga2_ref,
                        g3_ref, g4_ref, g5_ref, wout_ref, wgu_ref, wdn_ref, y_ref, act_ref):
    sbw = sbn_ref.shape[1]
    o = _dot(sbn_ref[...], wout_ref[0:sbw, :]) + _dot(mgn_ref[...], wout_ref[sbw:, :])
    x2 = x_ref[...] + ga1_ref[...] * _rms(o, g3_ref[...])
    y_ref[...] = _ffn(x2, sh2_ref[...], sc2_ref[...], ga2_ref[...], g4_ref[...], g5_ref[...],
                      wgu_ref, wdn_ref, act_ref)


def _mix_out_ffn_call(sample, x, sbn, mgn, mod, rows_per_mod, norm_g3, wout, wgu, wdn, tm):
    n, d = x.shape
    d_ff = wdn.shape[0]
    row = lambda w: pl.BlockSpec((tm, w), lambda i: (i, 0))
    in_specs = ([row(d), row(sbn.shape[1]), row(mgn.shape[1])]
                + _mod_specs(mod, (5, 6, 7, 8), tm, rows_per_mod)
                + _norm_specs((3, 4, 5), d)
                + [_resident(a.shape) for a in (wout, wgu, wdn)])
    return pl.pallas_call(
        _mix_out_ffn_kernel,
        grid=(n // tm,),
        in_specs=in_specs,
        out_specs=row(d),
        out_shape=jax.ShapeDtypeStruct((n, d), F32),
        scratch_shapes=[pltpu.VMEM((tm, d_ff), BF16)],
        compiler_params=pltpu.CompilerParams(dimension_semantics=("arbitrary",),
                                             vmem_limit_bytes=VMEM_LIMIT_BYTES),
        name="mix_out_ffn_sample" if sample else "mix_out_ffn",
    )(x, sbn, mgn, mod, mod, mod, mod, norm_g3, norm_g3, norm_g3, wout, wgu, wdn)


def kernel(x_prompt, x_sample, c_prompt, c_sample, cache_k, cache_v, page_table, w_c, b_c, norm_g, w_in, sb_bias, mlp_v_g, w_s, b_s, sb_out_g, mlp_out_g, w_out, w1_gu, w1_dn, w2_gu, w2_dn):
    depth = w_c.shape[0]
    nb_p, seq, d = x_prompt.shape
    nb_s, dec_seq, _ = x_sample.shape
    n_sb, hd = sb_out_g.shape[1:]
    n_mlp = mlp_v_g.shape[1]
    sbw = n_sb * hd
    assert hd == HEAD_DIM and n_mlp * hd == sbw and dec_seq == 1
    assert w_s.shape[2] == CHUNK and cache_k.shape[2] == PAGE_SIZE
    assert seq % ATTN_TILE == 0 and seq % ROW_TILE == 0 and ROW_TILE % CHUNK == 0

    head_id = jnp.arange(sbw) // HEAD_DIM
    seg = (head_id[:, None] == head_id[None, :]).astype(BF16)
    flat = lambda a: a.reshape(1, sbw)

    xp = x_prompt.reshape(nb_p * seq, d)
    xs = x_sample.reshape(nb_s, d)
    c_all = jnp.concatenate([c_prompt, c_sample], axis=0)
    outs = [[] for _ in range(5)]
    for l in range(depth):
        mod = _mod_call(c_all, w_c[l].astype(BF16), b_c[l])
        mod_p = mod[:nb_p].reshape(nb_p, 1, -1)
        mod_s = mod[nb_p:]
        norm_g3 = norm_g[l].reshape(2 * N_SUB, 1, d)
        w1 = (w1_gu[l].astype(BF16), w1_dn[l].astype(BF16))
        w2 = (w2_gu[l].astype(BF16), w2_dn[l].astype(BF16))
        win = w_in[l].astype(BF16)
        wout = w_out[l].astype(BF16)
        vg, og, sg = flat(mlp_v_g[l]), flat(mlp_out_g[l]), flat(sb_out_g[l])
        bs_rows = jnp.repeat(b_s[l].T, HEAD_DIM, axis=1)
        ws0 = flat(jnp.repeat(w_s[l][:, 0, 0], HEAD_DIM))
        bs0 = bs_rows[0:1]

        x1p, ktp, vtp, qbp, ktbp, vbp, mgnp = _ffn_mix_in_call(
            False, xp, mod_p, seq, norm_g3, *w1, win, vg, og, seg, w_s[l], bs_rows, ROW_TILE)
        x1s, ks, vs, qbs, gvs, mgns = _ffn_mix_in_call(
            True, xs, mod_s, 1, norm_g3, *w1, win, vg, og, seg, ws0, bs0, nb_s)

        sbnp, sbns = _sb_call(qbp, ktbp, vbp, sb_bias[l], sg, seq, page_table, qbs, seg,
                              cache_k[l], cache_v[l])

        xp = _mix_out_ffn_call(False, x1p, sbnp, mgnp, mod_p, seq, norm_g3, wout, *w2, ROW_TILE)
        xs = _mix_out_ffn_call(True, x1s, sbns, mgns, mod_s, 1, norm_g3, wout, *w2, nb_s)

        outs[0].append(ktp.reshape(nb_p, n_sb, hd, seq).transpose(0, 3, 1, 2))
        outs[1].append(vtp.reshape(nb_p, n_sb, hd, seq).transpose(0, 3, 1, 2))
        outs[2].append(ks.reshape(nb_s, 1, n_sb, hd))
        outs[3].append(vs.reshape(nb_s, 1, n_sb, hd))
        outs[4].append(gvs.reshape(nb_s, 1, n_mlp, hd))
    return (xp.reshape(nb_p, seq, d), xs.reshape(nb_s, 1, d)) + tuple(jnp.stack(o) for o in outs)
```
